```python
import jax, jax.numpy as jnp
from jax import lax
import numpy as np

D_MODEL = 1024
BATCH = 8
SEQ = 4096
DEPTH = 2

GRID_W = 64
CTX_LEN = 256
EPS = 1e-6
NEG_INF = -1e30

LRU_WIDTH = D_MODEL
LRU_HEADS = 16
LRU_BLOCK = LRU_WIDTH // LRU_HEADS
LRU_C = 8.0
CONV_W = 4
CONV_LEFT = 2
FNET_WIDTH = D_MODEL
FNET_GROUPS = 8
FNET_GROUP_DIM = FNET_WIDTH // FNET_GROUPS
L0_IN_WIDTH = 2 * LRU_WIDTH + 2 * FNET_WIDTH
L0_MIX_WIDTH = LRU_WIDTH + FNET_WIDTH
HEAD_DIM = 64
N_Q_HEADS = D_MODEL // HEAD_DIM
N_KV_HEADS = 4
GQA_GROUP = N_Q_HEADS // N_KV_HEADS
Q_WIDTH = N_Q_HEADS * HEAD_DIM
KV_WIDTH = N_KV_HEADS * HEAD_DIM
L1_IN_WIDTH = 2 * Q_WIDTH + 2 * KV_WIDTH
WINDOW = 128
BLOCK = 128
ROPE_BASE = 10000.0

kernel_name = "hybrid_rglru_fourier_window_gqa_ctx_prefix"


def rms_norm(x, g):
    xf = x.astype(jnp.float32)
    y = xf * lax.rsqrt(jnp.mean(xf * xf, axis=-1, keepdims=True) + EPS)
    return (y * g.astype(jnp.float32)).astype(x.dtype)


def modulation(cvec, w_mod, b_mod):
    m = jax.nn.silu(cvec) @ w_mod + b_mod
    return jnp.split(m, 3, axis=-1)


def centred_depthwise_conv(u, w, b):
    n = u.shape[1]
    up = jnp.pad(u, ((0, 0), (CONV_LEFT, CONV_W - 1 - CONV_LEFT), (0, 0)))
    out = b + up[:, 0:n] * w[0]
    for k in range(1, CONV_W):
        out = out + up[:, k:k + n] * w[k]
    return out


def block_diag_linear(u, w, b):
    bsz, n, _ = u.shape
    y = jnp.einsum('bnhd,hde->bnhe', u.reshape(bsz, n, LRU_HEADS, LRU_BLOCK), w)
    return y.reshape(bsz, n, LRU_WIDTH) + b


def rglru_coeffs(u, w_a, b_a, w_x, b_x, lam):
    uf = u.astype(jnp.float32)
    r = jax.nn.sigmoid(block_diag_linear(uf, w_a.astype(jnp.float32), b_a.astype(jnp.float32)))
    i = jax.nn.sigmoid(block_diag_linear(uf, w_x.astype(jnp.float32), b_x.astype(jnp.float32)))
    log_a = -LRU_C * r * jax.nn.softplus(-lam.astype(jnp.float32))
    a = jnp.exp(log_a)
    bterm = jnp.sqrt(-jnp.expm1(2.0 * log_a)) * (i * uf)
    return a, bterm


def linear_scan(a, b, h0):
    b = b.at[:, 0].add(a[:, 0] * h0)

    def combine(lft, rgt):
        return (lft[0] * rgt[0], rgt[0] * lft[1] + rgt[1])

    _, h = lax.associative_scan(combine, (a, b), axis=1)
    return h


def rglru_bidirectional(u_lat, u_ctx, w_a, b_a, w_x, b_x, lam):
    y_lat = None
    y_ctx = None
    for d in range(2):
        ul = u_lat if d == 0 else jnp.flip(u_lat, axis=1)
        uc = u_ctx if d == 0 else jnp.flip(u_ctx, axis=1)
        a_c, b_c = rglru_coeffs(uc, w_a[d], b_a[d], w_x[d], b_x[d], lam[d])
        h_c = linear_scan(a_c, b_c, jnp.zeros_like(b_c[:, 0]))
        a_l, b_l = rglru_coeffs(ul, w_a[d], b_a[d], w_x[d], b_x[d], lam[d])
        h_l = linear_scan(a_l, b_l, h_c[:, -1])
        if d == 1:
            h_c = jnp.flip(h_c, axis=1)
            h_l = jnp.flip(h_l, axis=1)
        y_lat = h_l if y_lat is None else y_lat + h_l
        y_ctx = h_c if y_ctx is None else y_ctx + h_c
    return y_lat, y_ctx


def fourier_mix(u, w_f, b_f):
    bsz, n, _ = u.shape
    ug = u.astype(jnp.float32).reshape(bsz, n, FNET_GROUPS, FNET_GROUP_DIM)
    f = jnp.fft.fft2(ug, axes=(1, 3), norm="ortho").real
    y = jnp.einsum('bngd,gde->bnge', f, w_f.astype(jnp.float32)).reshape(bsz, n, FNET_WIDTH)
    return (y + b_f.astype(jnp.float32)).astype(u.dtype)


def lru_fourier_layer(h_lat, h_ctx, w_in, w_conv, b_conv, w_a, b_a, w_x, b_x, lam,
                      w_f, b_f, w_out, with_ctx_out):
    o1, o2, o3 = LRU_WIDTH, 2 * LRU_WIDTH, 2 * LRU_WIDTH + FNET_WIDTH
    z = h_lat @ w_in
    u_l, g_l, f_l, gf_l = z[..., :o1], z[..., o1:o2], z[..., o2:o3], z[..., o3:]
    if with_ctx_out:
        zc = h_ctx @ w_in
        u_c, g_c, f_c, gf_c = zc[..., :o1], zc[..., o1:o2], zc[..., o2:o3], zc[..., o3:]
    else:
        u_c = h_ctx @ w_in[:, :o1]
    u_l = centred_depthwise_conv(u_l, w_conv, b_conv)
    u_c = centred_depthwise_conv(u_c, w_conv, b_conv)
    r_l, r_c = rglru_bidirectional(u_l, u_c, w_a, b_a, w_x, b_x, lam)
    mix_l = jnp.concatenate([r_l * jax.nn.silu(g_l), fourier_mix(f_l, w_f, b_f) * jax.nn.silu(gf_l)], axis=-1)
    y_lat = mix_l @ w_out
    y_ctx = None
    if with_ctx_out:
        mix_c = jnp.concatenate([r_c * jax.nn.silu(g_c), fourier_mix(f_c, w_f, b_f) * jax.nn.silu(gf_c)], axis=-1)
        y_ctx = mix_c @ w_out
    return y_lat, y_ctx


def axial_rope_tables(n):
    t = jnp.arange(n)
    row = (t // GRID_W).astype(jnp.float32)
    col = (t % GRID_W).astype(jnp.float32)
    half = HEAD_DIM // 2
    freqs = ROPE_BASE ** (-jnp.arange(0, half, 2, dtype=jnp.float32) / half)
    ar = row[:, None] * freqs[None, :]
    ac = col[:, None] * freqs[None, :]
    return jnp.cos(ar), jnp.sin(ar), jnp.cos(ac), jnp.sin(ac)


def rope_1d(x, cos, sin):
    x1, x2 = jnp.split(x, 2, axis=-1)
    cos = cos[None, :, None, :].astype(x.dtype)
    sin = sin[None, :, None, :].astype(x.dtype)
    return jnp.concatenate([x1 * cos - x2 * sin, x2 * cos + x1 * sin], axis=-1)


def axial_rope(x, cos_r, sin_r, cos_c, sin_c):
    half = HEAD_DIM // 2
    return jnp.concatenate([rope_1d(x[..., :half], cos_r, sin_r),
                            rope_1d(x[..., half:], cos_c, sin_c)], axis=-1)


def window_gqa_layer(h_lat, h_ctx, w_in, sink, w_out, with_ctx_out):
    bsz, n, _ = h_lat.shape
    n_ctx = h_ctx.shape[1]
    nb = n // BLOCK
    scale = HEAD_DIM ** -0.5
    o1, o2, o3 = Q_WIDTH, Q_WIDTH + KV_WIDTH, Q_WIDTH + 2 * KV_WIDTH
    z = h_lat @ w_in
    q = z[..., :o1].reshape(bsz, n, N_Q_HEADS, HEAD_DIM)
    k = z[..., o1:o2].reshape(bsz, n, N_KV_HEADS, HEAD_DIM)
    v = z[..., o2:o3].reshape(bsz, n, N_KV_HEADS, HEAD_DIM)
    g = z[..., o3:]
    cos_r, sin_r, cos_c, sin_c = axial_rope_tables(n)
    q = axial_rope(q, cos_r, sin_r, cos_c, sin_c)
    k = axial_rope(k, cos_r, sin_r, cos_c, sin_c)
    zc = h_ctx @ w_in[:, o1:o3]
    k_c = zc[..., :KV_WIDTH].reshape(bsz, n_ctx, N_KV_HEADS, HEAD_DIM)
    v_c = zc[..., KV_WIDTH:].reshape(bsz, n_ctx, N_KV_HEADS, HEAD_DIM)
    sink_f = sink.astype(jnp.float32).reshape(1, 1, N_KV_HEADS, GQA_GROUP, 1, 1)

    qb = q.reshape(bsz, nb, BLOCK, N_KV_HEADS, GQA_GROUP, HEAD_DIM)
    pad = ((0, 0), (BLOCK, BLOCK), (0, 0), (0, 0))
    kp = jnp.pad(k, pad).reshape(bsz, nb + 2, BLOCK, N_KV_HEADS, HEAD_DIM)
    vp = jnp.pad(v, pad).reshape(bsz, nb + 2, BLOCK, N_KV_HEADS, HEAD_DIM)
    kw = jnp.concatenate([kp[:, :-2], kp[:, 1:-1], kp[:, 2:]], axis=2)
    vw = jnp.concatenate([vp[:, :-2], vp[:, 1:-1], vp[:, 2:]], axis=2)
    s_lat = jnp.einsum('bnqkgd,bnskd->bnkgqs', qb, kw).astype(jnp.float32) * scale
    blk = jnp.arange(nb)[:, None, None]
    qpos = blk * BLOCK + jnp.arange(BLOCK)[None, :, None]
    kpos = (blk - 1) * BLOCK + jnp.arange(3 * BLOCK)[None, None, :]
    valid = (jnp.abs(qpos - kpos) <= WINDOW) & (kpos >= 0) & (kpos < n)
    s_lat = jnp.where(valid[None, :, None, None], s_lat, NEG_INF)
    s_ctx = jnp.einsum('bnqkgd,bskd->bnkgqs', qb, k_c).astype(jnp.float32) * scale
    s_sink = jnp.broadcast_to(sink_f, s_lat.shape[:-1] + (1,))
    probs = jax.nn.softmax(jnp.concatenate([s_lat, s_ctx, s_sink], axis=-1), axis=-1)
    p_lat = probs[..., :3 * BLOCK].astype(v.dtype)
    p_ctx = probs[..., 3 * BLOCK:3 * BLOCK + n_ctx].astype(v.dtype)
    o = (jnp.einsum('bnkgqs,bnskd->bnqkgd', p_lat, vw)
         + jnp.einsum('bnkgqs,bskd->bnqkgd', p_ctx, v_c)).reshape(bsz, n, Q_WIDTH)
    y_lat = (o * jax.nn.silu(g)) @ w_out

    y_ctx = None
    if with_ctx_out:
        q_c = (h_ctx @ w_in[:, :o1]).reshape(bsz, n_ctx, N_KV_HEADS, GQA_GROUP, HEAD_DIM)
        g_c = h_ctx @ w_in[:, o3:]
        s_c = jnp.einsum('bqkgd,bskd->bkgqs', q_c, k_c).astype(jnp.float32) * scale
        s_cs = jnp.broadcast_to(sink_f[0], s_c.shape[:-1] + (1,))
        pr = jax.nn.softmax(jnp.concatenate([s_c, s_cs], axis=-1), axis=-1)[..., :n_ctx]
        o_c = jnp.einsum('bkgqs,bskd->bqkgd', pr.astype(v_c.dtype), v_c).reshape(bsz, n_ctx, Q_WIDTH)
        y_ctx = (o_c * jax.nn.silu(g_c)) @ w_out
    return y_lat, y_ctx


def setup_inputs(seed: int = 0) -> dict:
    key = jax.random.key(seed)
    ks = jax.random.split(key, 32)
    f32 = jnp.float32
    d = D_MODEL

    def nrm(k, shape, s):
        return jax.random.normal(k, shape, f32) * s

    a0 = jax.random.uniform(ks[13], (2, LRU_WIDTH), f32, 0.9, 0.999)
    return {
        "x": nrm(ks[0], (BATCH, SEQ, d), 1.0),
        "c": nrm(ks[1], (BATCH, d), 1.0),
        "ctx": nrm(ks[2], (BATCH, CTX_LEN, d), 1.0),
        "c_ctx": nrm(ks[3], (d,), 1.0),
        "l0_w_mod": nrm(ks[4], (d, 3 * d), 0.5 * d ** -0.5),
        "l0_b_mod": nrm(ks[5], (3 * d,), 0.01),
        "l0_g_pre": 1.0 + nrm(ks[6], (d,), 0.05),
        "l0_g_post": 1.0 + nrm(ks[7], (d,), 0.05),
        "l0_w_in": nrm(ks[8], (d, L0_IN_WIDTH), d ** -0.5),
        "l0_w_conv": nrm(ks[9], (CONV_W, LRU_WIDTH), CONV_W ** -0.5),
        "l0_b_conv": nrm(ks[10], (LRU_WIDTH,), 0.01),
        "l0_w_a": nrm(ks[11], (2, LRU_HEADS, LRU_BLOCK, LRU_BLOCK), LRU_BLOCK ** -0.5),
        "l0_b_a": nrm(ks[12], (2, LRU_WIDTH), 0.01),
        "l0_w_x": nrm(ks[14], (2, LRU_HEADS, LRU_BLOCK, LRU_BLOCK), LRU_BLOCK ** -0.5),
        "l0_b_x": nrm(ks[15], (2, LRU_WIDTH), 0.01),
        "l0_lam": jnp.log(a0) - jnp.log1p(-a0),
        "l0_w_f": nrm(ks[16], (FNET_GROUPS, FNET_GROUP_DIM, FNET_GROUP_DIM), FNET_GROUP_DIM ** -0.5),
        "l0_b_f": nrm(ks[17], (FNET_WIDTH,), 0.01),
        "l0_w_out": nrm(ks[18], (L0_MIX_WIDTH, d), L0_MIX_WIDTH ** -0.5),
        "l1_w_mod": nrm(ks[19], (d, 3 * d), 0.5 * d ** -0.5),
        "l1_b_mod": nrm(ks[20], (3 * d,), 0.01),
        "l1_g_pre": 1.0 + nrm(ks[21], (d,), 0.05),
        "l1_g_post": 1.0 + nrm(ks[22], (d,), 0.05),
        "l1_w_in": nrm(ks[23], (d, L1_IN_WIDTH), d ** -0.5),
        "l1_sink": nrm(ks[24], (N_Q_HEADS,), 1.0),
        "l1_w_out": nrm(ks[25], (Q_WIDTH, d), Q_WIDTH ** -0.5),
    }


def reference(x, c, ctx, c_ctx,
              l0_w_mod, l0_b_mod, l0_g_pre, l0_g_post, l0_w_in, l0_w_conv, l0_b_conv,
              l0_w_a, l0_b_a, l0_w_x, l0_b_x, l0_lam, l0_w_f, l0_b_f, l0_w_out,
              l1_w_mod, l1_b_mod, l1_g_pre, l1_g_post, l1_w_in, l1_sink, l1_w_out):
    layers = (
        dict(w_mod=l0_w_mod, b_mod=l0_b_mod, g_pre=l0_g_pre, g_post=l0_g_post),
        dict(w_mod=l1_w_mod, b_mod=l1_b_mod, g_pre=l1_g_pre, g_post=l1_g_post),
    )
    for layer in range(DEPTH):
        p = layers[layer]
        with_ctx_out = layer < DEPTH - 1
        sh_l, sc_l, gt_l = modulation(c, p["w_mod"], p["b_mod"])
        sh_c, sc_c, gt_c = modulation(c_ctx, p["w_mod"], p["b_mod"])
        h_lat = rms_norm(x, p["g_pre"]) * (1.0 + sc_l[:, None, :]) + sh_l[:, None, :]
        h_ctx = rms_norm(ctx, p["g_pre"]) * (1.0 + sc_c) + sh_c
        if layer % 2 == 0:
            y_lat, y_ctx = lru_fourier_layer(h_lat, h_ctx, l0_w_in, l0_w_conv, l0_b_conv,
                                             l0_w_a, l0_b_a, l0_w_x, l0_b_x, l0_lam,
                                             l0_w_f, l0_b_f, l0_w_out, with_ctx_out)
        else:
            y_lat, y_ctx = window_gqa_layer(h_lat, h_ctx, l1_w_in, l1_sink, l1_w_out, with_ctx_out)
        x = x + gt_l[:, None, :] * rms_norm(y_lat, p["g_post"])
        if with_ctx_out:
            ctx = ctx + gt_c * rms_norm(y_ctx, p["g_post"])
    return x
```

```python
import functools
import math

import jax
import jax.numpy as jnp
import numpy as np
from jax import lax
from jax.experimental import pallas as pl
from jax.experimental.pallas import tpu as pltpu

EPS = 1e-6
NEG_INF = -1e30
LRU_HEADS = 16
LRU_C = 8.0
CONV_W = 4
CONV_LEFT = 2
FNET_GROUPS = 8
HEAD_DIM = 64
N_KV_HEADS = 4
WINDOW = 128
ATTN_BLOCK = 128
GRID_W = 64
ROPE_BASE = 10000.0

LANES = 128
SUBLANES = 8
BF16_ROWS = 16
GATE_COLS = 256
VMEM_LIMIT = 56 * 1024 * 1024

BF16 = jnp.bfloat16
F32 = jnp.float32


def _params(*sem):
    return pltpu.CompilerParams(dimension_semantics=sem, vmem_limit_bytes=VMEM_LIMIT)


def _sigmoid(z):
    return 0.5 * jnp.tanh(0.5 * z) + 0.5


def _silu(z):
    return z * _sigmoid(z)


def _rms(v, g):
    return v * lax.rsqrt(jnp.mean(v * v, axis=-1, keepdims=True) + EPS) * g


def _row_tile(n, want):
    t = min(n, want)
    assert n % t == 0
    return t


def _mod_body(c_ref, w_ref, b_ref, o_ref):
    o_ref[...] = jnp.dot(_silu(c_ref[...]), w_ref[...], preferred_element_type=F32,
                         precision=lax.Precision.HIGHEST) + b_ref[...]


def _modulation(cvec, w_mod, b_mod):
    rows, d = cvec.shape
    width = w_mod.shape[1]
    tn = _row_tile(width, 1024)
    return pl.pallas_call(
        _mod_body,
        grid=(width // tn,),
        in_specs=[pl.BlockSpec((rows, d), lambda j: (0, 0)),
                  pl.BlockSpec((d, tn), lambda j: (0, j)),
                  pl.BlockSpec((1, tn), lambda j: (0, j))],
        out_specs=pl.BlockSpec((rows, tn), lambda j: (0, j)),
        out_shape=jax.ShapeDtypeStruct((rows, width), F32),
        compiler_params=_params("arbitrary"),
        name="modulation",
    )(cvec, w_mod, b_mod.reshape(1, width))


def _fold_body(win_ref, wf_ref, cd_ref, sd_ref, wp_ref, wq_ref):
    hi = lax.Precision.HIGHEST
    win = win_ref[...]
    wf = wf_ref[0]
    cw = jnp.dot(cd_ref[...], wf, preferred_element_type=F32, precision=hi)
    sw = jnp.dot(sd_ref[...], wf, preferred_element_type=F32, precision=hi)
    wp_ref[...] = jnp.dot(win, cw, preferred_element_type=F32, precision=hi)
    wq_ref[...] = jnp.dot(win, sw, preferred_element_type=F32, precision=hi)


def _fold_fourier_weights(w_in, w_f, col0):
    d = w_in.shape[0]
    groups, gd, _ = w_f.shape
    idx = np.arange(gd)
    ang = 2.0 * np.pi * ((idx[:, None] * idx[None, :]) % gd) / gd
    cd = jnp.asarray(np.cos(ang) / math.sqrt(gd), F32)
    sd = jnp.asarray(np.sin(ang) / math.sqrt(gd), F32)
    out = jax.ShapeDtypeStruct((d, groups * gd), F32)
    return pl.pallas_call(
        _fold_body,
        grid=(groups,),
        in_specs=[pl.BlockSpec((d, gd), lambda g: (0, col0 // gd + g)),
                  pl.BlockSpec((1, gd, gd), lambda g: (g, 0, 0)),
                  pl.BlockSpec((gd, gd), lambda g: (0, 0)),
                  pl.BlockSpec((gd, gd), lambda g: (0, 0))],
        out_specs=[pl.BlockSpec((d, gd), lambda g: (0, g)),
                   pl.BlockSpec((d, gd), lambda g: (0, g))],
        out_shape=[out, out],
        compiler_params=_params("arbitrary"),
        name="fold_fourier_weights",
    )(w_in, w_f, cd, sd)


def _l0_in_body(x_ref, sc_ref, sh_ref, g_ref, w_ref, u_ref, sg_ref, p_ref, q_ref, sgf_ref):
    width = u_ref.shape[-1]
    h = _rms(x_ref[0], g_ref[...]) * (1.0 + sc_ref[0]) + sh_ref[0]
    hb = h.astype(BF16)
    outs = (u_ref, sg_ref, p_ref, q_ref, sgf_ref)
    gated = (False, True, False, False, True)
    for part, (o_ref, is_gate) in enumerate(zip(outs, gated)):
        z = jnp.dot(hb, w_ref[:, part * width:(part + 1) * width], preferred_element_type=F32)
        if is_gate:
            z = _silu(z)
        o_ref[0] = z.astype(o_ref.dtype)


def _l0_in_proj(x, sc, sh, g_pre, w_cat, per_batch_mod):
    bsz, n, d = x.shape
    width = w_cat.shape[1] // 5
    t = _row_tile(n, 512)
    mod_idx = (lambda b, j: (b, 0, 0)) if per_batch_mod else (lambda b, j: (0, 0, 0))
    out = jax.ShapeDtypeStruct((bsz, n, width), BF16)
    row_spec = pl.BlockSpec((1, t, width), lambda b, j: (b, j, 0))
    return pl.pallas_call(
        _l0_in_body,
        grid=(bsz, n // t),
        in_specs=[pl.BlockSpec((1, t, d), lambda b, j: (b, j, 0)),
                  pl.BlockSpec((1, 1, d), mod_idx),
                  pl.BlockSpec((1, 1, d), mod_idx),
                  pl.BlockSpec((1, d), lambda b, j: (0, 0)),
                  pl.BlockSpec(w_cat.shape, lambda b, j: (0, 0))],
        out_specs=[row_spec] * 5,
        out_shape=[out] * 5,
        compiler_params=_params("parallel", "arbitrary"),
        name="l0_in_proj",
    )(x, sc, sh, g_pre.reshape(1, d), w_cat)


def _lru_direction(rev, scan_start, seq_first, seq_last, u_ref, up_ref, un_ref, wconv_ref,
                   bconv_ref, wg_ref, ba_ref, bx_ref, lam_ref, h0_ref, y_ref, hfin_ref,
                   ext_ref, a_ref, b_ref, carry_ref):
    t, width = a_ref.shape
    d = 1 if rev else 0

    prev = up_ref[0, BF16_ROWS - SUBLANES:, :].astype(F32)
    nxt = un_ref[0, :SUBLANES, :].astype(F32)
    ext_ref[0:SUBLANES, :] = jnp.where(seq_first, 0.0, prev)
    ext_ref[SUBLANES:SUBLANES + t, :] = u_ref[0].astype(F32)
    ext_ref[SUBLANES + t:, :] = jnp.where(seq_last, 0.0, nxt)

    z = -lam_ref[d:d + 1, :]
    c_lam = -LRU_C * (jnp.maximum(z, 0.0) + jnp.log1p(jnp.exp(-jnp.abs(z))))

    for cg in range(width // GATE_COLS):
        cols = slice(cg * GATE_COLS, (cg + 1) * GATE_COLS)
        uc = bconv_ref[:, cols] + jnp.zeros((t, GATE_COLS), F32)
        for k in range(CONV_W):
            start = SUBLANES + k - CONV_LEFT
            uc = uc + ext_ref[start:start + t, cols] * wconv_ref[k:k + 1, cols]
        zg = jnp.dot(uc.astype(BF16), wg_ref[d, cg], preferred_element_type=F32)
        r = _sigmoid(zg[:, :GATE_COLS] + ba_ref[d:d + 1, cols])
        i = _sigmoid(zg[:, GATE_COLS:] + bx_ref[d:d + 1, cols])
        log_a = c_lam[:, cols] * r
        a = jnp.exp(log_a)
        amp = jnp.sqrt(-jnp.tanh(log_a) * (1.0 + a * a))
        a_ref[:, cols] = a
        b_ref[:, cols] = amp * (i * uc)

    @pl.when(scan_start)
    def _():
        carry_ref[d:d + 1, :] = h0_ref[0]

    row = lax.broadcasted_iota(jnp.int32, (SUBLANES, width), 0)
    groups = t // SUBLANES

    def body(it, h):
        g = (groups - 1 - it) if rev else it
        rows = pl.ds(pl.multiple_of(g * SUBLANES, SUBLANES), SUBLANES)
        av = a_ref[rows, :]
        bv = b_ref[rows, :]
        for s in (1, 2, 4):
            if rev:
                shift, m = SUBLANES - s, row < SUBLANES - s
            else:
                shift, m = s, row >= s
            a_sh = pltpu.roll(av, shift, 0)
            b_sh = pltpu.roll(bv, shift, 0)
            bv = jnp.where(m, av * b_sh + bv, bv)
            av = jnp.where(m, av * a_sh, av)
        hrows = av * h + bv
        b_ref[rows, :] = hrows
        return hrows[0:1, :] if rev else hrows[SUBLANES - 1:SUBLANES, :]

    h_last = lax.fori_loop(0, groups, body, carry_ref[d:d + 1, :])
    carry_ref[d:d + 1, :] = h_last
    hfin_ref[0] = h_last
    y_ref[0] = b_ref[...].astype(y_ref.dtype)


def _lru_body(uf_ref, ufp_ref, ufn_ref, ub_ref, ubp_ref, ubn_ref, wconv_ref, bconv_ref, wg_ref,
              ba_ref, bx_ref, lam_ref, h0f_ref, h0b_ref,
              yf_ref, yb_ref, hf_ref, hb_ref, ext_ref, a_ref, b_ref, carry_ref):
    j = pl.program_id(1)
    nt = pl.num_programs(1)
    common = (wconv_ref, bconv_ref, wg_ref, ba_ref, bx_ref, lam_ref)
    scratch = (ext_ref, a_ref, b_ref, carry_ref)
    first, last = j == 0, j == nt - 1
    _lru_direction(False, first, first, last, uf_ref, ufp_ref, ufn_ref, *common,
                   h0f_ref, yf_ref, hf_ref, *scratch)
    _lru_direction(True, first, last, first, ub_ref, ubp_ref, ubn_ref, *common,
                   h0b_ref, yb_ref, hb_ref, *scratch)


def _rglru(u, w_conv, b_conv, wg, b_a, b_x, lam, h0f, h0b):
    bsz, n, width = u.shape
    t = _row_tile(n, 512)
    nt = n // t
    hb_per = t // BF16_ROWS
    last_h = n // BF16_ROWS - 1

    def main(b, j):
        return (b, j, 0)

    def prev(b, j):
        return (b, jnp.maximum(j * hb_per - 1, 0), 0)

    def nxt(b, j):
        return (b, jnp.minimum((j + 1) * hb_per, last_h), 0)

    def rev(f):
        return lambda b, j: f(b, nt - 1 - j)

    tile = (1, t, width)
    halo = (1, BF16_ROWS, width)
    const2 = lambda b, j: (0, 0)
    state = pl.BlockSpec((1, 1, width), lambda b, j: (b, 0, 0))
    y_shape = jax.ShapeDtypeStruct((bsz, n, width), BF16)
    h_shape = jax.ShapeDtypeStruct((bsz, 1, width), F32)
    return pl.pallas_call(
        _lru_body,
        grid=(bsz, nt),
        in_specs=[pl.BlockSpec(tile, main), pl.BlockSpec(halo, prev), pl.BlockSpec(halo, nxt),
                  pl.BlockSpec(tile, rev(main)), pl.BlockSpec(halo, rev(prev)),
                  pl.BlockSpec(halo, rev(nxt)),
                  pl.BlockSpec(w_conv.shape, const2),
                  pl.BlockSpec((1, width), const2),
                  pl.BlockSpec(wg.shape, lambda b, j: (0, 0, 0, 0)),
                  pl.BlockSpec(b_a.shape, const2),
                  pl.BlockSpec(b_x.shape, const2),
                  pl.BlockSpec(lam.shape, const2),
                  state, state],
        out_specs=[pl.BlockSpec(tile, main), pl.BlockSpec(tile, rev(main)), state, state],
        out_shape=[y_shape, y_shape, h_shape, h_shape],
        scratch_shapes=[pltpu.VMEM((t + 2 * SUBLANES, width), F32),
                        pltpu.VMEM((t, width), F32),
                        pltpu.VMEM((t, width), F32),
                        pltpu.VMEM((2, width), F32)],
        compiler_params=_params("parallel", "arbitrary"),
        name="rglru_scan",
    )(u, u, u, u, u, u, w_conv, b_conv.reshape(1, width), wg, b_a, b_x, lam, h0f, h0b)


def _gate_weights(w_a, w_x):
    _, heads, blk, _ = w_a.shape
    per = GATE_COLS // blk
    groups = heads // per
    eye = jnp.eye(per, dtype=w_a.dtype)

    def bd(w):
        w = w.reshape(2, groups, per, blk, blk)
        full = jnp.einsum('dgpij,pq->dgpiqj', w, eye)
        return full.reshape(2, groups, GATE_COLS, GATE_COLS)

    return jnp.concatenate([bd(w_a), bd(w_x)], axis=-1).astype(BF16)


def _dft_body(c_ref, s_ref, p_ref, q_ref, bf_ref, sgf_ref, o_ref, acc_ref):
    k = pl.program_id(2)

    @pl.when(k == 0)
    def _():
        acc_ref[...] = jnp.zeros_like(acc_ref)

    acc_ref[...] += (jnp.dot(c_ref[...], p_ref[0], preferred_element_type=F32)
                     + jnp.dot(s_ref[...], q_ref[0], preferred_element_type=F32))

    @pl.when(k == pl.num_programs(2) - 1)
    def _():
        o_ref[0] = ((acc_ref[...] + bf_ref[...]) * sgf_ref[0].astype(F32)).astype(o_ref.dtype)


def _dft_matrices(n):
    n1 = 1
    while n1 * n1 < n:
        n1 *= 2
    n2 = n // n1
    k = jnp.arange(n, dtype=jnp.int32)[:, None]
    hi = jnp.arange(n2, dtype=jnp.int32)[None, :]
    lo = jnp.arange(n1, dtype=jnp.int32)[None, :]
    ang_hi = ((k * hi * n1) % n).astype(F32) * (2.0 * math.pi / n)
    ang_lo = ((k * lo) % n).astype(F32) * (2.0 * math.pi / n)
    ch, sh_ = jnp.cos(ang_hi)[:, :, None], jnp.sin(ang_hi)[:, :, None]
    cl, sl = jnp.cos(ang_lo)[:, None, :], jnp.sin(ang_lo)[:, None, :]
    scale = n ** -0.5
    cmat = ((ch * cl - sh_ * sl) * scale).reshape(n, n).astype(BF16)
    smat = ((sh_ * cl + ch * sl) * (-scale)).reshape(n, n).astype(BF16)
    return cmat, smat


def _fourier_positions(p, q, b_f, sgf):
    bsz, n, width = p.shape
    cmat, smat = _dft_matrices(n)
    tm = _row_tile(n, 1024)
    tk = _row_tile(n, 1024)
    return pl.pallas_call(
        _dft_body,
        grid=(bsz, n // tm, n // tk),
        in_specs=[pl.BlockSpec((tm, tk), lambda b, i, k: (i, k)),
                  pl.BlockSpec((tm, tk), lambda b, i, k: (i, k)),
                  pl.BlockSpec((1, tk, width), lambda b, i, k: (b, k, 0)),
                  pl.BlockSpec((1, tk, width), lambda b, i, k: (b, k, 0)),
                  pl.BlockSpec((1, width), lambda b, i, k: (0, 0)),
                  pl.BlockSpec((1, tm, width), lambda b, i, k: (b, i, 0))],
        out_specs=pl.BlockSpec((1, tm, width), lambda b, i, k: (b, i, 0)),
        out_shape=jax.ShapeDtypeStruct((bsz, n, width), BF16),
        scratch_shapes=[pltpu.VMEM((tm, width), F32)],
        compiler_params=_params("parallel", "parallel", "arbitrary"),
        name="fourier_positions",
    )(cmat, smat, p, q, b_f.reshape(1, width), sgf)


def _l0_out_body(yf_ref, yb_ref, sg_ref, fm_ref, w_ref, g_ref, gt_ref, x_ref, o_ref):
    width = yf_ref.shape[-1]
    r = (yf_ref[0].astype(F32) + yb_ref[0].astype(F32)) * sg_ref[0].astype(F32)
    y = (jnp.dot(r.astype(BF16), w_ref[:width, :], preferred_element_type=F32)
         + jnp.dot(fm_ref[0], w_ref[width:, :], preferred_element_type=F32))
    o_ref[0] = x_ref[0] + gt_ref[0] * _rms(y, g_ref[...])


def _l0_out_proj(yf, yb, sg, fmix, w_out, g_post, gt, x, per_batch_mod):
    bsz, n, width = yf.shape
    d = x.shape[-1]
    t = _row_tile(n, 512)
    mod_idx = (lambda b, j: (b, 0, 0)) if per_batch_mod else (lambda b, j: (0, 0, 0))
    row = lambda b, j: (b, j, 0)
    return pl.pallas_call(
        _l0_out_body,
        grid=(bsz, n // t),
        in_specs=[pl.BlockSpec((1, t, width), row)] * 4
                 + [pl.BlockSpec(w_out.shape, lambda b, j: (0, 0)),
                    pl.BlockSpec((1, d), lambda b, j: (0, 0)),
                    pl.BlockSpec((1, 1, d), mod_idx),
                    pl.BlockSpec((1, t, d), row)],
        out_specs=pl.BlockSpec((1, t, d), row),
        out_shape=jax.ShapeDtypeStruct((bsz, n, d), F32),
        compiler_params=_params("parallel", "arbitrary"),
        name="l0_out_proj",
    )(yf, yb, sg, fmix, w_out, g_post.reshape(1, d), gt, x)


def _rope(z, cos, sin_signed):
    quarter = HEAD_DIM // 4
    lane = lax.broadcasted_iota(jnp.int32, (z.shape[0], LANES), 1)
    first = (lane % (2 * quarter)) < quarter
    chunks = []
    for c in range(z.shape[1] // LANES):
        zc = z[:, c * LANES:(c + 1) * LANES]
        swapped = jnp.where(first, pltpu.roll(zc, LANES - quarter, 1), pltpu.roll(zc, quarter, 1))
        chunks.append(zc * cos + swapped * sin_signed)
    return jnp.concatenate(chunks, axis=1)


def _l1_in_body(x_ref, sc_ref, sh_ref, g_ref, w_ref, cos_ref, sin_ref, q_ref, k_ref, v_ref, sg_ref):
    qw, kw = q_ref.shape[-1], k_ref.shape[-1]
    h = _rms(x_ref[0], g_ref[...]) * (1.0 + sc_ref[0]) + sh_ref[0]
    hb = h.astype(BF16)
    cos, sin = cos_ref[...], sin_ref[...]
    zq = jnp.dot(hb, w_ref[:, :qw], preferred_element_type=F32)
    q_ref[0] = (_rope(zq, cos, sin) * (HEAD_DIM ** -0.5)).astype(q_ref.dtype)
    zk = jnp.dot(hb, w_ref[:, qw:qw + kw], preferred_element_type=F32)
    k_ref[0] = _rope(zk, cos, sin).astype(k_ref.dtype)
    zv = jnp.dot(hb, w_ref[:, qw + kw:qw + 2 * kw], preferred_element_type=F32)
    v_ref[0] = zv.astype(v_ref.dtype)
    zg = jnp.dot(hb, w_ref[:, qw + 2 * kw:], preferred_element_type=F32)
    sg_ref[0] = _silu(zg).astype(sg_ref.dtype)


def _rope_tables(n):
    t = jnp.arange(n)
    row = (t // GRID_W).astype(F32)
    col = (t % GRID_W).astype(F32)
    half = HEAD_DIM // 2
    freqs = ROPE_BASE ** (-jnp.arange(0, half, 2, dtype=F32) / half)
    ar = row[:, None] * freqs[None, :]
    ac = col[:, None] * freqs[None, :]
    cos = jnp.concatenate([jnp.cos(ar), jnp.cos(ar), jnp.cos(ac), jnp.cos(ac)], axis=1)
    sin = jnp.concatenate([-jnp.sin(ar), jnp.sin(ar), -jnp.sin(ac), jnp.sin(ac)], axis=1)
    reps = LANES // HEAD_DIM
    return jnp.tile(cos, (1, reps)), jnp.tile(sin, (1, reps))


def _l1_in_proj(x, sc, sh, g_pre, w_in, q_width, kv_width):
    bsz, n, d = x.shape
    t = _row_tile(n, 512)
    cos, sin = _rope_tables(n)
    row = lambda b, j: (b, j, 0)
    mod_idx = lambda b, j: (b, 0, 0)
    return pl.pallas_call(
        _l1_in_body,
        grid=(bsz, n // t),
        in_specs=[pl.BlockSpec((1, t, d), row),
                  pl.BlockSpec((1, 1, d), mod_idx),
                  pl.BlockSpec((1, 1, d), mod_idx),
                  pl.BlockSpec((1, d), lambda b, j: (0, 0)),
                  pl.BlockSpec(w_in.shape, lambda b, j: (0, 0)),
                  pl.BlockSpec((t, LANES), lambda b, j: (j, 0)),
                  pl.BlockSpec((t, LANES), lambda b, j: (j, 0))],
        out_specs=[pl.BlockSpec((1, t, q_width), row),
                   pl.BlockSpec((1, t, kv_width), row),
                   pl.BlockSpec((1, t, kv_width), row),
                   pl.BlockSpec((1, t, q_width), row)],
        out_shape=[jax.ShapeDtypeStruct((bsz, n, q_width), BF16),
                   jax.ShapeDtypeStruct((bsz, n, kv_width), BF16),
                   jax.ShapeDtypeStruct((bsz, n, kv_width), BF16),
                   jax.ShapeDtypeStruct((bsz, n, q_width), BF16)],
        compiler_params=_params("parallel", "arbitrary"),
        name="l1_in_proj",
    )(x, sc, sh, g_pre.reshape(1, d), w_in, cos, sin)


def _ctx_kv_body(x_ref, sc_ref, sh_ref, g_ref, w_ref, k_ref, v_ref):
    kw = k_ref.shape[-1]
    h = _rms(x_ref[0], g_ref[...]) * (1.0 + sc_ref[0]) + sh_ref[0]
    z = jnp.dot(h.astype(BF16), w_ref[...], preferred_element_type=F32)
    k_ref[0] = z[:, :kw].astype(k_ref.dtype)
    v_ref[0] = z[:, kw:].astype(v_ref.dtype)


def _ctx_kv_proj(ctx, sc, sh, g_pre, w_kv):
    bsz, n, d = ctx.shape
    kw = w_kv.shape[1] // 2
    row = lambda b: (b, 0, 0)
    shared = lambda b: (0, 0, 0)
    out = jax.ShapeDtypeStruct((bsz, n, kw), BF16)
    return pl.pallas_call(
        _ctx_kv_body,
        grid=(bsz,),
        in_specs=[pl.BlockSpec((1, n, d), row),
                  pl.BlockSpec((1, 1, d), shared),
                  pl.BlockSpec((1, 1, d), shared),
                  pl.BlockSpec((1, d), lambda b: (0, 0)),
                  pl.BlockSpec(w_kv.shape, lambda b: (0, 0))],
        out_specs=[pl.BlockSpec((1, n, kw), row)] * 2,
        out_shape=[out, out],
        compiler_params=_params("arbitrary"),
        name="ctx_kv_proj",
    )(ctx, sc, sh, g_pre.reshape(1, d), w_kv)


def _attn_body(sink_ref, q_ref, km_ref, kp_ref, kn_ref, vm_ref, vp_ref, vn_ref, kc_ref, vc_ref,
               sg_ref, w_ref, g_ref, gt_ref, x_ref, o_ref, kf_ref, vf_ref, att_ref):
    i = pl.program_id(1)
    ni = pl.num_programs(1)
    tq = q_ref.shape[1]
    blk = ATTN_BLOCK
    per_step = tq // blk
    group = q_ref.shape[-1] // HEAD_DIM // N_KV_HEADS

    kf_ref[0:blk, :] = kp_ref[0]
    kf_ref[blk:blk + tq, :] = km_ref[0]
    kf_ref[blk + tq:, :] = kn_ref[0]
    vf_ref[0:blk, :] = vp_ref[0]
    vf_ref[blk:blk + tq, :] = vm_ref[0]
    vf_ref[blk + tq:, :] = vn_ref[0]

    ri = lax.broadcasted_iota(jnp.int32, (blk, 3 * blk), 0)
    cj = lax.broadcasted_iota(jnp.int32, (blk, 3 * blk), 1)
    window_bias = jnp.where(jnp.abs(cj - blk - ri) <= WINDOW, 0.0, NEG_INF)

    for qb in range(per_step):
        r0 = qb * blk
        prev_bias = jnp.where((i * per_step + qb) > 0, 0.0, NEG_INF)
        next_bias = jnp.where((i * per_step + qb) < ni * per_step - 1, 0.0, NEG_INF)
        bias = (window_bias + jnp.where(cj < blk, prev_bias, 0.0)
                + jnp.where(cj >= 2 * blk, next_bias, 0.0))
        for kh in range(N_KV_HEADS):
            hs = slice(kh * HEAD_DIM, (kh + 1) * HEAD_DIM)
            q4 = jnp.concatenate(
                [q_ref[0, r0:r0 + blk, (kh * group + g) * HEAD_DIM:(kh * group + g + 1) * HEAD_DIM]
                 for g in range(group)], axis=0)
            k_lat = kf_ref[r0:r0 + 3 * blk, hs]
            v_lat = vf_ref[r0:r0 + 3 * blk, hs]
            nt_dims = (((1,), (1,)), ((), ()))
            s_lat = lax.dot_general(q4, k_lat, nt_dims, preferred_element_type=F32)
            s_ctx = lax.dot_general(q4, kc_ref[0, :, hs], nt_dims, preferred_element_type=F32)
            outs = []
            for g in range(group):
                sink = sink_ref[kh * group + g]
                sl = s_lat[g * blk:(g + 1) * blk] + bias
                sc = s_ctx[g * blk:(g + 1) * blk]
                m = jnp.maximum(jnp.maximum(jnp.max(sl, axis=-1, keepdims=True),
                                            jnp.max(sc, axis=-1, keepdims=True)), sink)
                pl_ = jnp.exp(sl - m)
                pc = jnp.exp(sc - m)
                denom = (jnp.sum(pl_, axis=-1, keepdims=True) + jnp.sum(pc, axis=-1, keepdims=True)
                         + jnp.exp(sink - m))
                og = (jnp.dot(pl_.astype(BF16), v_lat, preferred_element_type=F32)
                      + jnp.dot(pc.astype(BF16), vc_ref[0, :, hs], preferred_element_type=F32))
                outs.append(og / denom)
            att_ref[r0:r0 + blk, kh * group * HEAD_DIM:(kh + 1) * group * HEAD_DIM] = (
                jnp.concatenate(outs, axis=1))

    gated = (att_ref[...] * sg_ref[0].astype(F32)).astype(BF16)
    y = jnp.dot(gated, w_ref[...], preferred_element_type=F32)
    o_ref[0] = x_ref[0] + gt_ref[0] * _rms(y, g_ref[...])


def _attention_out(q, k, v, kc, vc, sg, sink, w_out, g_post, gt, x):
    bsz, n, qw = q.shape
    kw = k.shape[-1]
    nc = kc.shape[1]
    d = x.shape[-1]
    blk = ATTN_BLOCK
    tq = _row_tile(n, 512)
    per = tq // blk
    nb = n // blk
    row = lambda b, i: (b, i, 0)
    prev = lambda b, i: (b, jnp.maximum(i * per - 1, 0), 0)
    nxt = lambda b, i: (b, jnp.minimum((i + 1) * per, nb - 1), 0)
    batch = lambda b, i: (b, 0, 0)
    kv_specs = [pl.BlockSpec((1, tq, kw), row), pl.BlockSpec((1, blk, kw), prev),
                pl.BlockSpec((1, blk, kw), nxt)]
    return pl.pallas_call(
        _attn_body,
        grid=(bsz, n // tq),
        in_specs=[pl.BlockSpec(memory_space=pltpu.SMEM),
                  pl.BlockSpec((1, tq, qw), row)]
                 + kv_specs + kv_specs
                 + [pl.BlockSpec((1, nc, kw), batch), pl.BlockSpec((1, nc, kw), batch),
                    pl.BlockSpec((1, tq, qw), row),
                    pl.BlockSpec(w_out.shape, lambda b, i: (0, 0)),
                    pl.BlockSpec((1, d), lambda b, i: (0, 0)),
                    pl.BlockSpec((1, 1, d), batch),
                    pl.BlockSpec((1, tq, d), row)],
        out_specs=pl.BlockSpec((1, tq, d), row),
        out_shape=jax.ShapeDtypeStruct((bsz, n, d), F32),
        scratch_shapes=[pltpu.VMEM((tq + 2 * blk, kw), BF16),
                        pltpu.VMEM((tq + 2 * blk, kw), BF16),
                        pltpu.VMEM((tq, qw), F32)],
        compiler_params=_params("parallel", "arbitrary"),
        name="window_attention_out",
    )(sink, q, k, k, k, v, v, v, kc, vc, sg, w_out, g_post.reshape(1, d), gt, x)


def _mod_vectors(c, c_ctx, w_mod, b_mod):
    bsz, d = c.shape
    rows = -(-(bsz + 1) // SUBLANES) * SUBLANES
    cvec = jnp.zeros((rows, d), F32).at[:bsz].set(c).at[bsz].set(c_ctx)
    m = _modulation(cvec, w_mod, b_mod)
    lat = [m[:bsz, k * d:(k + 1) * d].reshape(bsz, 1, d) for k in range(3)]
    cx = [m[bsz:bsz + 1, k * d:(k + 1) * d].reshape(1, 1, d) for k in range(3)]
    return lat, cx


def kernel(x, c, ctx, c_ctx, l0_w_mod, l0_b_mod, l0_g_pre, l0_g_post, l0_w_in, l0_w_conv, l0_b_conv,
           l0_w_a, l0_b_a, l0_w_x, l0_b_x, l0_lam, l0_w_f, l0_b_f, l0_w_out,
           l1_w_mod, l1_b_mod, l1_g_pre, l1_g_post, l1_w_in, l1_sink, l1_w_out):
    bsz, n, d = x.shape
    lru_w = l0_w_conv.shape[1]
    f_w = l0_w_f.shape[0] * l0_w_f.shape[1]
    assert lru_w == f_w == d

    (sh_l, sc_l, gt_l), (sh_c, sc_c, gt_c) = _mod_vectors(c, c_ctx, l0_w_mod, l0_b_mod)
    wp, wq = _fold_fourier_weights(l0_w_in, l0_w_f, 2 * lru_w)
    w_cat = jnp.concatenate([l0_w_in[:, :2 * lru_w], wp, wq, l0_w_in[:, 2 * lru_w + f_w:]],
                            axis=1).astype(BF16)
    wg = _gate_weights(l0_w_a, l0_w_x)
    w_out0 = l0_w_out.astype(BF16)

    u_c, sg_c, p_c, q_c, sgf_c = _l0_in_proj(ctx, sc_c, sh_c, l0_g_pre, w_cat, False)
    u_l, sg_l, p_l, q_l, sgf_l = _l0_in_proj(x, sc_l, sh_l, l0_g_pre, w_cat, True)

    zeros = jnp.zeros((bsz, 1, lru_w), F32)
    lru_args = (l0_w_conv, l0_b_conv, wg, l0_b_a, l0_b_x, l0_lam)
    yf_c, yb_c, hf_c, hb_c = _rglru(u_c, *lru_args, zeros, zeros)
    yf_l, yb_l, _, _ = _rglru(u_l, *lru_args, hf_c, hb_c)

    fm_c = _fourier_positions(p_c, q_c, l0_b_f, sgf_c)
    fm_l = _fourier_positions(p_l, q_l, l0_b_f, sgf_l)

    ctx1 = _l0_out_proj(yf_c, yb_c, sg_c, fm_c, w_out0, l0_g_post, gt_c, ctx, False)
    x1 = _l0_out_proj(yf_l, yb_l, sg_l, fm_l, w_out0, l0_g_post, gt_l, x, True)

    (sh_l, sc_l, gt_l), (sh_c, sc_c, _) = _mod_vectors(c, c_ctx, l1_w_mod, l1_b_mod)
    q_width = l1_w_out.shape[0]
    kv_width = (l1_w_in.shape[1] - 2 * q_width) // 2
    w_in1 = l1_w_in.astype(BF16)
    kc, vc = _ctx_kv_proj(ctx1, sc_c, sh_c, l1_g_pre, w_in1[:, q_width:q_width + 2 * kv_width])
    q, k, v, sg1 = _l1_in_proj(x1, sc_l, sh_l, l1_g_pre, w_in1, q_width, kv_width)
    return _attention_out(q, k, v, kc, vc, sg1, l1_sink, l1_w_out.astype(BF16), l1_g_post, gt_l, x1)
```

```python
import functools
import math

import jax
import jax.numpy as jnp
import numpy as np
from jax import lax
from jax.experimental import pallas as pl
from jax.experimental.pallas import tpu as pltpu

EPS = 1e-6
NEG_INF = -1e30
LRU_HEADS = 16
LRU_C = 8.0
CONV_W = 4
CONV_LEFT = 2
FNET_GROUPS = 8
HEAD_DIM = 64
N_KV_HEADS = 4
WINDOW = 128
ATTN_BLOCK = 128
GRID_W = 64
ROPE_BASE = 10000.0

LANES = 128
SUBLANES = 8
BF16_ROWS = 16
FFT_RADIX = 16
GATE_COLS = 256
VMEM_LIMIT = 56 * 1024 * 1024

BF16 = jnp.bfloat16
F32 = jnp.float32


def _params(*sem):
    return pltpu.CompilerParams(dimension_semantics=sem, vmem_limit_bytes=VMEM_LIMIT)


def _sigmoid(z):
    return 0.5 * jnp.tanh(0.5 * z) + 0.5


def _silu(z):
    return z * _sigmoid(z)


def _rms(v, g):
    return v * lax.rsqrt(jnp.mean(v * v, axis=-1, keepdims=True) + EPS) * g


def _row_tile(n, want):
    t = min(n, want)
    assert n % t == 0
    return t


def _mod_body(c_ref, w_ref, b_ref, o_ref):
    o_ref[...] = jnp.dot(_silu(c_ref[...]), w_ref[...], preferred_element_type=F32,
                         precision=lax.Precision.HIGHEST) + b_ref[...]


def _modulation(cvec, w_mod, b_mod):
    rows, d = cvec.shape
    width = w_mod.shape[1]
    tn = _row_tile(width, 1024)
    return pl.pallas_call(
        _mod_body,
        grid=(width // tn,),
        in_specs=[pl.BlockSpec((rows, d), lambda j: (0, 0)),
                  pl.BlockSpec((d, tn), lambda j: (0, j)),
                  pl.BlockSpec((1, tn), lambda j: (0, j))],
        out_specs=pl.BlockSpec((rows, tn), lambda j: (0, j)),
        out_shape=jax.ShapeDtypeStruct((rows, width), F32),
        compiler_params=_params("arbitrary"),
        name="modulation",
    )(cvec, w_mod, b_mod.reshape(1, width))


def _fold_body(win_ref, wf_ref, cd_ref, sd_ref, wp_ref, wq_ref):
    hi = lax.Precision.HIGHEST
    win = win_ref[...]
    wf = wf_ref[0]
    cw = jnp.dot(cd_ref[...], wf, preferred_element_type=F32, precision=hi)
    sw = jnp.dot(sd_ref[...], wf, preferred_element_type=F32, precision=hi)
    wp_ref[...] = jnp.dot(win, cw, preferred_element_type=F32, precision=hi)
    wq_ref[...] = jnp.dot(win, sw, preferred_element_type=F32, precision=hi)


def _fold_fourier_weights(w_in, w_f, col0):
    d = w_in.shape[0]
    groups, gd, _ = w_f.shape
    idx = np.arange(gd)
    ang = 2.0 * np.pi * ((idx[:, None] * idx[None, :]) % gd) / gd
    cd = jnp.asarray(np.cos(ang) / math.sqrt(gd), F32)
    sd = jnp.asarray(np.sin(ang) / math.sqrt(gd), F32)
    out = jax.ShapeDtypeStruct((d, groups * gd), F32)
    return pl.pallas_call(
        _fold_body,
        grid=(groups,),
        in_specs=[pl.BlockSpec((d, gd), lambda g: (0, col0 // gd + g)),
                  pl.BlockSpec((1, gd, gd), lambda g: (g, 0, 0)),
                  pl.BlockSpec((gd, gd), lambda g: (0, 0)),
                  pl.BlockSpec((gd, gd), lambda g: (0, 0))],
        out_specs=[pl.BlockSpec((d, gd), lambda g: (0, g)),
                   pl.BlockSpec((d, gd), lambda g: (0, g))],
        out_shape=[out, out],
        compiler_params=_params("arbitrary"),
        name="fold_fourier_weights",
    )(w_in, w_f, cd, sd)


def _l0_in_body(decimate, x_ref, sc_ref, sh_ref, g_ref, w_ref, u_ref, sg_ref, p_ref, q_ref, sgf_ref,
                *slab):
    width = u_ref.shape[-1]
    h = _rms(x_ref[0], g_ref[...]) * (1.0 + sc_ref[0]) + sh_ref[0]
    hb = h.astype(BF16)
    outs = (u_ref, sg_ref, p_ref, q_ref, sgf_ref)
    gated = (False, True, False, False, True)
    for part, (o_ref, is_gate) in enumerate(zip(outs, gated)):
        z = jnp.dot(hb, w_ref[:, part * width:(part + 1) * width], preferred_element_type=F32)
        if is_gate:
            z = _silu(z)
        if decimate and part in (2, 3):
            slab_ref, = slab
            per = o_ref.shape[2]
            for c in range(width // LANES):
                slab_ref[c] = z[:, c * LANES:(c + 1) * LANES]
            for n1 in range(FFT_RADIX):
                for c in range(width // LANES):
                    rows = slab_ref[c, pl.ds(n1, per, stride=FFT_RADIX), :]
                    o_ref[0, n1, :, c * LANES:(c + 1) * LANES] = rows.astype(o_ref.dtype)
        else:
            o_ref[0] = z.astype(o_ref.dtype)


def _l0_in_proj(x, sc, sh, g_pre, w_cat, per_batch_mod, decimate):
    bsz, n, d = x.shape
    width = w_cat.shape[1] // 5
    t = _row_tile(n, 512)
    mod_idx = (lambda b, j: (b, 0, 0)) if per_batch_mod else (lambda b, j: (0, 0, 0))
    out = jax.ShapeDtypeStruct((bsz, n, width), BF16)
    row_spec = pl.BlockSpec((1, t, width), lambda b, j: (b, j, 0))
    if decimate:
        dec = jax.ShapeDtypeStruct((bsz, FFT_RADIX, n // FFT_RADIX, width), BF16)
        dec_spec = pl.BlockSpec((1, FFT_RADIX, t // FFT_RADIX, width), lambda b, j: (b, 0, j, 0))
        out_specs = [row_spec, row_spec, dec_spec, dec_spec, row_spec]
        out_shape = [out, out, dec, dec, out]
        scratch = [pltpu.VMEM((width // LANES, t, LANES), F32)]
    else:
        out_specs, out_shape, scratch = [row_spec] * 5, [out] * 5, []
    return pl.pallas_call(
        functools.partial(_l0_in_body, decimate),
        grid=(bsz, n // t),
        in_specs=[pl.BlockSpec((1, t, d), lambda b, j: (b, j, 0)),
                  pl.BlockSpec((1, 1, d), mod_idx),
                  pl.BlockSpec((1, 1, d), mod_idx),
                  pl.BlockSpec((1, d), lambda b, j: (0, 0)),
                  pl.BlockSpec(w_cat.shape, lambda b, j: (0, 0))],
        out_specs=out_specs,
        out_shape=out_shape,
        scratch_shapes=scratch,
        compiler_params=_params("parallel", "arbitrary"),
        name="l0_in_proj",
    )(x, sc, sh, g_pre.reshape(1, d), w_cat)


def _lru_direction(rev, scan_start, seq_first, seq_last, u_ref, up_ref, un_ref, wconv_ref,
                   bconv_ref, wg_ref, ba_ref, bx_ref, lam_ref, h0_ref, y_ref, hfin_ref,
                   ext_ref, a_ref, b_ref, carry_ref):
    t, width = a_ref.shape
    d = 1 if rev else 0

    prev = up_ref[0, BF16_ROWS - SUBLANES:, :].astype(F32)
    nxt = un_ref[0, :SUBLANES, :].astype(F32)
    ext_ref[0:SUBLANES, :] = jnp.where(seq_first, 0.0, prev)
    ext_ref[SUBLANES:SUBLANES + t, :] = u_ref[0].astype(F32)
    ext_ref[SUBLANES + t:, :] = jnp.where(seq_last, 0.0, nxt)

    z = -lam_ref[d:d + 1, :]
    c_lam = -LRU_C * (jnp.maximum(z, 0.0) + jnp.log1p(jnp.exp(-jnp.abs(z))))

    for cg in range(width // GATE_COLS):
        cols = slice(cg * GATE_COLS, (cg + 1) * GATE_COLS)
        uc = bconv_ref[:, cols] + jnp.zeros((t, GATE_COLS), F32)
        for k in range(CONV_W):
            start = SUBLANES + k - CONV_LEFT
            uc = uc + ext_ref[start:start + t, cols] * wconv_ref[k:k + 1, cols]
        zg = jnp.dot(uc.astype(BF16), wg_ref[d, cg], preferred_element_type=F32)
        r = _sigmoid(zg[:, :GATE_COLS] + ba_ref[d:d + 1, cols])
        i = _sigmoid(zg[:, GATE_COLS:] + bx_ref[d:d + 1, cols])
        log_a = c_lam[:, cols] * r
        a = jnp.exp(log_a)
        amp = jnp.sqrt(-jnp.tanh(log_a) * (1.0 + a * a))
        a_ref[:, cols] = a
        b_ref[:, cols] = amp * (i * uc)

    @pl.when(scan_start)
    def _():
        carry_ref[d:d + 1, :] = h0_ref[0]

    row = lax.broadcasted_iota(jnp.int32, (SUBLANES, width), 0)
    groups = t // SUBLANES

    def body(it, h):
        g = (groups - 1 - it) if rev else it
        rows = pl.ds(pl.multiple_of(g * SUBLANES, SUBLANES), SUBLANES)
        av = a_ref[rows, :]
        bv = b_ref[rows, :]
        for s in (1, 2, 4):
            if rev:
                shift, m = SUBLANES - s, row < SUBLANES - s
            else:
                shift, m = s, row >= s
            a_sh = pltpu.roll(av, shift, 0)
            b_sh = pltpu.roll(bv, shift, 0)
            bv = jnp.where(m, av * b_sh + bv, bv)
            av = jnp.where(m, av * a_sh, av)
        hrows = av * h + bv
        b_ref[rows, :] = hrows
        return hrows[0:1, :] if rev else hrows[SUBLANES - 1:SUBLANES, :]

    h_last = lax.fori_loop(0, groups, body, carry_ref[d:d + 1, :])
    carry_ref[d:d + 1, :] = h_last
    hfin_ref[0] = h_last
    y_ref[0] = b_ref[...].astype(y_ref.dtype)


def _lru_body(uf_ref, ufp_ref, ufn_ref, ub_ref, ubp_ref, ubn_ref, wconv_ref, bconv_ref, wg_ref,
              ba_ref, bx_ref, lam_ref, h0f_ref, h0b_ref,
              yf_ref, yb_ref, hf_ref, hb_ref, ext_ref, a_ref, b_ref, carry_ref):
    j = pl.program_id(1)
    nt = pl.num_programs(1)
    common = (wconv_ref, bconv_ref, wg_ref, ba_ref, bx_ref, lam_ref)
    scratch = (ext_ref, a_ref, b_ref, carry_ref)
    first, last = j == 0, j == nt - 1
    _lru_direction(False, first, first, last, uf_ref, ufp_ref, ufn_ref, *common,
                   h0f_ref, yf_ref, hf_ref, *scratch)
    _lru_direction(True, first, last, first, ub_ref, ubp_ref, ubn_ref, *common,
                   h0b_ref, yb_ref, hb_ref, *scratch)


def _rglru(u, w_conv, b_conv, wg, b_a, b_x, lam, h0f, h0b):
    bsz, n, width = u.shape
    t = _row_tile(n, 512)
    nt = n // t
    hb_per = t // BF16_ROWS
    last_h = n // BF16_ROWS - 1

    def main(b, j):
        return (b, j, 0)

    def prev(b, j):
        return (b, jnp.maximum(j * hb_per - 1, 0), 0)

    def nxt(b, j):
        return (b, jnp.minimum((j + 1) * hb_per, last_h), 0)

    def rev(f):
        return lambda b, j: f(b, nt - 1 - j)

    tile = (1, t, width)
    halo = (1, BF16_ROWS, width)
    const2 = lambda b, j: (0, 0)
    state = pl.BlockSpec((1, 1, width), lambda b, j: (b, 0, 0))
    y_shape = jax.ShapeDtypeStruct((bsz, n, width), BF16)
    h_shape = jax.ShapeDtypeStruct((bsz, 1, width), F32)
    return pl.pallas_call(
        _lru_body,
        grid=(bsz, nt),
        in_specs=[pl.BlockSpec(tile, main), pl.BlockSpec(halo, prev), pl.BlockSpec(halo, nxt),
                  pl.BlockSpec(tile, rev(main)), pl.BlockSpec(halo, rev(prev)),
                  pl.BlockSpec(halo, rev(nxt)),
                  pl.BlockSpec(w_conv.shape, const2),
                  pl.BlockSpec((1, width), const2),
                  pl.BlockSpec(wg.shape, lambda b, j: (0, 0, 0, 0)),
                  pl.BlockSpec(b_a.shape, const2),
                  pl.BlockSpec(b_x.shape, const2),
                  pl.BlockSpec(lam.shape, const2),
                  state, state],
        out_specs=[pl.BlockSpec(tile, main), pl.BlockSpec(tile, rev(main)), state, state],
        out_shape=[y_shape, y_shape, h_shape, h_shape],
        scratch_shapes=[pltpu.VMEM((t + 2 * SUBLANES, width), F32),
                        pltpu.VMEM((t, width), F32),
                        pltpu.VMEM((t, width), F32),
                        pltpu.VMEM((2, width), F32)],
        compiler_params=_params("parallel", "arbitrary"),
        name="rglru_scan",
    )(u, u, u, u, u, u, w_conv, b_conv.reshape(1, width), wg, b_a, b_x, lam, h0f, h0b)


def _gate_weights(w_a, w_x):
    _, heads, blk, _ = w_a.shape
    per = GATE_COLS // blk
    groups = heads // per
    eye = jnp.eye(per, dtype=w_a.dtype)

    def bd(w):
        w = w.reshape(2, groups, per, blk, blk)
        full = jnp.einsum('dgpij,pq->dgpiqj', w, eye)
        return full.reshape(2, groups, GATE_COLS, GATE_COLS)

    return jnp.concatenate([bd(w_a), bd(w_x)], axis=-1).astype(BF16)


def _dft_body(c_ref, s_ref, p_ref, q_ref, bf_ref, sgf_ref, o_ref, acc_ref):
    k = pl.program_id(2)

    @pl.when(k == 0)
    def _():
        acc_ref[...] = jnp.zeros_like(acc_ref)

    acc_ref[...] += (jnp.dot(c_ref[...], p_ref[0], preferred_element_type=F32)
                     + jnp.dot(s_ref[...], q_ref[0], preferred_element_type=F32))

    @pl.when(k == pl.num_programs(2) - 1)
    def _():
        o_ref[0] = ((acc_ref[...] + bf_ref[...]) * sgf_ref[0].astype(F32)).astype(o_ref.dtype)


def _dft_matrices(n):
    n1 = 1
    while n1 * n1 < n:
        n1 *= 2
    n2 = n // n1
    m = jnp.arange(n, dtype=jnp.int32)[None, :]
    hi = jnp.arange(n2, dtype=jnp.int32)[:, None]
    lo = jnp.arange(n1, dtype=jnp.int32)[:, None]
    ang_hi = ((hi * n1 * m) % n).astype(F32) * (2.0 * math.pi / n)
    ang_lo = ((lo * m) % n).astype(F32) * (2.0 * math.pi / n)
    ch, sh_ = jnp.cos(ang_hi)[:, None, :], jnp.sin(ang_hi)[:, None, :]
    cl, sl = jnp.cos(ang_lo)[None, :, :], jnp.sin(ang_lo)[None, :, :]
    scale = n ** -0.5
    cmat = ((ch * cl - sh_ * sl) * scale).reshape(n, n).astype(BF16)
    smat = ((sh_ * cl + ch * sl) * (-scale)).reshape(n, n).astype(BF16)
    return cmat, smat


def _fourier_positions(p, q, b_f, sgf):
    bsz, n, width = p.shape
    cmat, smat = _dft_matrices(n)
    tm = _row_tile(n, 1024)
    tk = _row_tile(n, 1024)
    return pl.pallas_call(
        _dft_body,
        grid=(bsz, n // tm, n // tk),
        in_specs=[pl.BlockSpec((tm, tk), lambda b, i, k: (i, k)),
                  pl.BlockSpec((tm, tk), lambda b, i, k: (i, k)),
                  pl.BlockSpec((1, tk, width), lambda b, i, k: (b, k, 0)),
                  pl.BlockSpec((1, tk, width), lambda b, i, k: (b, k, 0)),
                  pl.BlockSpec((1, width), lambda b, i, k: (0, 0)),
                  pl.BlockSpec((1, tm, width), lambda b, i, k: (b, i, 0))],
        out_specs=pl.BlockSpec((1, tm, width), lambda b, i, k: (b, i, 0)),
        out_shape=jax.ShapeDtypeStruct((bsz, n, width), BF16),
        scratch_shapes=[pltpu.VMEM((tm, width), F32)],
        compiler_params=_params("parallel", "parallel", "arbitrary"),
        name="fourier_positions",
    )(cmat, smat, p, q, b_f.reshape(1, width), sgf)


def _fft_real(re, im):
    root_half = math.sqrt(0.5)

    def rec(re, im, want_im):
        n = len(re)
        if n == 1:
            return list(re), list(im)
        er, ei = rec(re[0::2], im[0::2], True)
        xr, xi = rec(re[1::2], im[1::2], True)
        out_r, out_i = [None] * n, [None] * n
        for k in range(n // 2):
            if k == 0:
                tr, ti = xr[k], xi[k]
            elif 4 * k == n:
                tr, ti = xi[k], None
            elif 8 * k == n:
                tr = (xr[k] + xi[k]) * root_half
                ti = (xi[k] - xr[k]) * root_half if want_im else None
            elif 8 * k == 3 * n:
                tr = (xi[k] - xr[k]) * root_half
                ti = (xr[k] + xi[k]) * (-root_half) if want_im else None
            else:
                c, s = math.cos(2 * math.pi * k / n), math.sin(2 * math.pi * k / n)
                tr = xr[k] * c + xi[k] * s
                ti = xi[k] * c - xr[k] * s if want_im else None
            out_r[k] = er[k] + tr
            out_r[k + n // 2] = er[k] - tr
            if want_im:
                if 4 * k == n:
                    out_i[k] = ei[k] - xr[k]
                    out_i[k + n // 2] = ei[k] + xr[k]
                else:
                    out_i[k] = ei[k] + ti
                    out_i[k + n // 2] = ei[k] - ti
        return out_r, out_i

    return rec(list(re), list(im), False)[0]


def _fft_body(p_ref, q_ref, wp_ref, wq_ref, bf_ref, sgf_ref, o_ref, yre_ref, yim_ref):
    radix, half, tc = yre_ref.shape
    for n1 in range(radix):
        y = (jnp.dot(wp_ref[n1], p_ref[0, n1], preferred_element_type=F32)
             + jnp.dot(wq_ref[n1], q_ref[0, n1], preferred_element_type=F32))
        yre_ref[n1] = y[:half]
        yim_ref[n1] = y[half:]

    rows = BF16_ROWS

    def chunk(it, carry):
        r0 = pl.multiple_of(it * rows, rows)
        for lt in range(tc // LANES):
            ls = slice(lt * LANES, (lt + 1) * LANES)
            re = [yre_ref[n1, pl.ds(r0, rows), ls] for n1 in range(radix)]
            im = [yim_ref[n1, pl.ds(r0, rows), ls] for n1 in range(radix)]
            out = _fft_real(re, im)
            for k1 in range(radix):
                gate = sgf_ref[0, k1, pl.ds(r0, rows), ls].astype(F32)
                o_ref[0, k1, pl.ds(r0, rows), ls] = ((out[k1] + bf_ref[:, ls]) * gate).astype(o_ref.dtype)
        return carry

    lax.fori_loop(0, half // rows, chunk, 0)


def _fft_matrices(n):
    half = n // FFT_RADIX
    k2 = jnp.arange(half, dtype=jnp.int32)[:, None]
    n1 = jnp.arange(FFT_RADIX, dtype=jnp.int32)[None, :]
    n2 = jnp.arange(half, dtype=jnp.int32)[None, :]
    ang1 = ((k2 * n1) % n).astype(F32) * (2.0 * math.pi / n)
    ang2 = ((k2 * n2) % half).astype(F32) * (2.0 * math.pi / half)
    c1, s1 = jnp.cos(ang1).T[:, :, None], jnp.sin(ang1).T[:, :, None]
    c2, s2 = jnp.cos(ang2)[None], jnp.sin(ang2)[None]
    scale = n ** -0.5
    cm = (c1 * c2 - s1 * s2) * scale
    sm = (s1 * c2 + c1 * s2) * scale
    wp = jnp.concatenate([cm, -sm], axis=1).astype(BF16)
    wq = jnp.concatenate([-sm, -cm], axis=1).astype(BF16)
    return wp, wq


def _fourier_positions_fft(pd, qd, b_f, sgf):
    bsz, radix, half, width = pd.shape
    n = radix * half
    wp, wq = _fft_matrices(n)
    tc = 2 * LANES
    data = pl.BlockSpec((1, radix, half, tc), lambda b, c: (b, 0, 0, c))
    mats = pl.BlockSpec((radix, 2 * half, half), lambda b, c: (0, 0, 0))
    out = pl.pallas_call(
        _fft_body,
        grid=(bsz, width // tc),
        in_specs=[data, data, mats, mats,
                  pl.BlockSpec((1, tc), lambda b, c: (0, c)),
                  data],
        out_specs=data,
        out_shape=jax.ShapeDtypeStruct((bsz, radix, half, width), BF16),
        scratch_shapes=[pltpu.VMEM((radix, half, tc), F32), pltpu.VMEM((radix, half, tc), F32)],
        compiler_params=_params("parallel", "arbitrary"),
        name="fourier_positions_fft",
    )(pd, qd, wp, wq, b_f.reshape(1, width), sgf.reshape(bsz, radix, half, width))
    return out.reshape(bsz, n, width)


def _l0_out_body(yf_ref, yb_ref, sg_ref, fm_ref, w_ref, g_ref, gt_ref, x_ref, o_ref):
    width = yf_ref.shape[-1]
    r = (yf_ref[0].astype(F32) + yb_ref[0].astype(F32)) * sg_ref[0].astype(F32)
    y = (jnp.dot(r.astype(BF16), w_ref[:width, :], preferred_element_type=F32)
         + jnp.dot(fm_ref[0], w_ref[width:, :], preferred_element_type=F32))
    o_ref[0] = x_ref[0] + gt_ref[0] * _rms(y, g_ref[...])


def _l0_out_proj(yf, yb, sg, fmix, w_out, g_post, gt, x, per_batch_mod):
    bsz, n, width = yf.shape
    d = x.shape[-1]
    t = _row_tile(n, 512)
    mod_idx = (lambda b, j: (b, 0, 0)) if per_batch_mod else (lambda b, j: (0, 0, 0))
    row = lambda b, j: (b, j, 0)
    return pl.pallas_call(
        _l0_out_body,
        grid=(bsz, n // t),
        in_specs=[pl.BlockSpec((1, t, width), row)] * 4
                 + [pl.BlockSpec(w_out.shape, lambda b, j: (0, 0)),
                    pl.BlockSpec((1, d), lambda b, j: (0, 0)),
                    pl.BlockSpec((1, 1, d), mod_idx),
                    pl.BlockSpec((1, t, d), row)],
        out_specs=pl.BlockSpec((1, t, d), row),
        out_shape=jax.ShapeDtypeStruct((bsz, n, d), F32),
        compiler_params=_params("parallel", "arbitrary"),
        name="l0_out_proj",
    )(yf, yb, sg, fmix, w_out, g_post.reshape(1, d), gt, x)


def _rope(z, cos, sin_signed):
    quarter = HEAD_DIM // 4
    lane = lax.broadcasted_iota(jnp.int32, (z.shape[0], LANES), 1)
    first = (lane % (2 * quarter)) < quarter
    chunks = []
    for c in range(z.shape[1] // LANES):
        zc = z[:, c * LANES:(c + 1) * LANES]
        swapped = jnp.where(first, pltpu.roll(zc, LANES - quarter, 1), pltpu.roll(zc, quarter, 1))
        chunks.append(zc * cos + swapped * sin_signed)
    return jnp.concatenate(chunks, axis=1)


def _l1_in_body(x_ref, sc_ref, sh_ref, g_ref, w_ref, cos_ref, sin_ref, q_ref, k_ref, v_ref, sg_ref):
    qw, kw = q_ref.shape[-1], k_ref.shape[-1]
    h = _rms(x_ref[0], g_ref[...]) * (1.0 + sc_ref[0]) + sh_ref[0]
    hb = h.astype(BF16)
    cos, sin = cos_ref[...], sin_ref[...]
    zq = jnp.dot(hb, w_ref[:, :qw], preferred_element_type=F32)
    q_ref[0] = (_rope(zq, cos, sin) * (HEAD_DIM ** -0.5)).astype(q_ref.dtype)
    zk = jnp.dot(hb, w_ref[:, qw:qw + kw], preferred_element_type=F32)
    k_ref[0] = _rope(zk, cos, sin).astype(k_ref.dtype)
    zv = jnp.dot(hb, w_ref[:, qw + kw:qw + 2 * kw], preferred_element_type=F32)
    v_ref[0] = zv.astype(v_ref.dtype)
    zg = jnp.dot(hb, w_ref[:, qw + 2 * kw:], preferred_element_type=F32)
    sg_ref[0] = _silu(zg).astype(sg_ref.dtype)


def _rope_tables(n):
    t = jnp.arange(n)
    row = (t // GRID_W).astype(F32)
    col = (t % GRID_W).astype(F32)
    half = HEAD_DIM // 2
    freqs = ROPE_BASE ** (-jnp.arange(0, half, 2, dtype=F32) / half)
    ar = row[:, None] * freqs[None, :]
    ac = col[:, None] * freqs[None, :]
    cos = jnp.concatenate([jnp.cos(ar), jnp.cos(ar), jnp.cos(ac), jnp.cos(ac)], axis=1)
    sin = jnp.concatenate([-jnp.sin(ar), jnp.sin(ar), -jnp.sin(ac), jnp.sin(ac)], axis=1)
    reps = LANES // HEAD_DIM
    return jnp.tile(cos, (1, reps)), jnp.tile(sin, (1, reps))


def _l1_in_proj(x, sc, sh, g_pre, w_in, q_width, kv_width):
    bsz, n, d = x.shape
    t = _row_tile(n, 512)
    cos, sin = _rope_tables(n)
    row = lambda b, j: (b, j, 0)
    mod_idx = lambda b, j: (b, 0, 0)
    return pl.pallas_call(
        _l1_in_body,
        grid=(bsz, n // t),
        in_specs=[pl.BlockSpec((1, t, d), row),
                  pl.BlockSpec((1, 1, d), mod_idx),
                  pl.BlockSpec((1, 1, d), mod_idx),
                  pl.BlockSpec((1, d), lambda b, j: (0, 0)),
                  pl.BlockSpec(w_in.shape, lambda b, j: (0, 0)),
                  pl.BlockSpec((t, LANES), lambda b, j: (j, 0)),
                  pl.BlockSpec((t, LANES), lambda b, j: (j, 0))],
        out_specs=[pl.BlockSpec((1, t, q_width), row),
                   pl.BlockSpec((1, t, kv_width), row),
                   pl.BlockSpec((1, t, kv_width), row),
                   pl.BlockSpec((1, t, q_width), row)],
        out_shape=[jax.ShapeDtypeStruct((bsz, n, q_width), BF16),
                   jax.ShapeDtypeStruct((bsz, n, kv_width), BF16),
                   jax.ShapeDtypeStruct((bsz, n, kv_width), BF16),
                   jax.ShapeDtypeStruct((bsz, n, q_width), BF16)],
        compiler_params=_params("parallel", "arbitrary"),
        name="l1_in_proj",
    )(x, sc, sh, g_pre.reshape(1, d), w_in, cos, sin)


def _ctx_kv_body(x_ref, sc_ref, sh_ref, g_ref, w_ref, k_ref, v_ref):
    kw = k_ref.shape[-1]
    h = _rms(x_ref[0], g_ref[...]) * (1.0 + sc_ref[0]) + sh_ref[0]
    z = jnp.dot(h.astype(BF16), w_ref[...], preferred_element_type=F32)
    k_ref[0] = z[:, :kw].astype(k_ref.dtype)
    v_ref[0] = z[:, kw:].astype(v_ref.dtype)


def _ctx_kv_proj(ctx, sc, sh, g_pre, w_kv):
    bsz, n, d = ctx.shape
    kw = w_kv.shape[1] // 2
    row = lambda b: (b, 0, 0)
    shared = lambda b: (0, 0, 0)
    out = jax.ShapeDtypeStruct((bsz, n, kw), BF16)
    return pl.pallas_call(
        _ctx_kv_body,
        grid=(bsz,),
        in_specs=[pl.BlockSpec((1, n, d), row),
                  pl.BlockSpec((1, 1, d), shared),
                  pl.BlockSpec((1, 1, d), shared),
                  pl.BlockSpec((1, d), lambda b: (0, 0)),
                  pl.BlockSpec(w_kv.shape, lambda b: (0, 0))],
        out_specs=[pl.BlockSpec((1, n, kw), row)] * 2,
        out_shape=[out, out],
        compiler_params=_params("arbitrary"),
        name="ctx_kv_proj",
    )(ctx, sc, sh, g_pre.reshape(1, d), w_kv)


def _attn_body(sink_ref, q_ref, km_ref, kp_ref, kn_ref, vm_ref, vp_ref, vn_ref, kc_ref, vc_ref,
               sg_ref, w_ref, g_ref, gt_ref, x_ref, o_ref, kf_ref, vf_ref, att_ref):
    i = pl.program_id(1)
    ni = pl.num_programs(1)
    tq = q_ref.shape[1]
    blk = ATTN_BLOCK
    per_step = tq // blk
    group = q_ref.shape[-1] // HEAD_DIM // N_KV_HEADS

    kf_ref[0:blk, :] = kp_ref[0]
    kf_ref[blk:blk + tq, :] = km_ref[0]
    kf_ref[blk + tq:, :] = kn_ref[0]
    vf_ref[0:blk, :] = vp_ref[0]
    vf_ref[blk:blk + tq, :] = vm_ref[0]
    vf_ref[blk + tq:, :] = vn_ref[0]

    ri = lax.broadcasted_iota(jnp.int32, (blk, 3 * blk), 0)
    cj = lax.broadcasted_iota(jnp.int32, (blk, 3 * blk), 1)
    window_bias = jnp.where(jnp.abs(cj - blk - ri) <= WINDOW, 0.0, NEG_INF)

    for qb in range(per_step):
        r0 = qb * blk
        prev_bias = jnp.where((i * per_step + qb) > 0, 0.0, NEG_INF)
        next_bias = jnp.where((i * per_step + qb) < ni * per_step - 1, 0.0, NEG_INF)
        bias = (window_bias + jnp.where(cj < blk, prev_bias, 0.0)
                + jnp.where(cj >= 2 * blk, next_bias, 0.0))
        for kh in range(N_KV_HEADS):
            hs = slice(kh * HEAD_DIM, (kh + 1) * HEAD_DIM)
            q4 = jnp.concatenate(
                [q_ref[0, r0:r0 + blk, (kh * group + g) * HEAD_DIM:(kh * group + g + 1) * HEAD_DIM]
                 for g in range(group)], axis=0)
            k_lat = kf_ref[r0:r0 + 3 * blk, hs]
            v_lat = vf_ref[r0:r0 + 3 * blk, hs]
            nt_dims = (((1,), (1,)), ((), ()))
            s_lat = lax.dot_general(q4, k_lat, nt_dims, preferred_element_type=F32)
            s_ctx = lax.dot_general(q4, kc_ref[0, :, hs], nt_dims, preferred_element_type=F32)
            outs = []
            for g in range(group):
                sink = sink_ref[kh * group + g]
                sl = s_lat[g * blk:(g + 1) * blk] + bias
                sc = s_ctx[g * blk:(g + 1) * blk]
                m = jnp.maximum(jnp.maximum(jnp.max(sl, axis=-1, keepdims=True),
                                            jnp.max(sc, axis=-1, keepdims=True)), sink)
                pl_ = jnp.exp(sl - m)
                pc = jnp.exp(sc - m)
                denom = (jnp.sum(pl_, axis=-1, keepdims=True) + jnp.sum(pc, axis=-1, keepdims=True)
                         + jnp.exp(sink - m))
                og = (jnp.dot(pl_.astype(BF16), v_lat, preferred_element_type=F32)
                      + jnp.dot(pc.astype(BF16), vc_ref[0, :, hs], preferred_element_type=F32))
                outs.append(og / denom)
            att_ref[r0:r0 + blk, kh * group * HEAD_DIM:(kh + 1) * group * HEAD_DIM] = (
                jnp.concatenate(outs, axis=1))

    gated = (att_ref[...] * sg_ref[0].astype(F32)).astype(BF16)
    y = jnp.dot(gated, w_ref[...], preferred_element_type=F32)
    o_ref[0] = x_ref[0] + gt_ref[0] * _rms(y, g_ref[...])


def _attention_out(q, k, v, kc, vc, sg, sink, w_out, g_post, gt, x):
    bsz, n, qw = q.shape
    kw = k.shape[-1]
    nc = kc.shape[1]
    d = x.shape[-1]
    blk = ATTN_BLOCK
    tq = _row_tile(n, 512)
    per = tq // blk
    nb = n // blk
    row = lambda b, i: (b, i, 0)
    prev = lambda b, i: (b, jnp.maximum(i * per - 1, 0), 0)
    nxt = lambda b, i: (b, jnp.minimum((i + 1) * per, nb - 1), 0)
    batch = lambda b, i: (b, 0, 0)
    kv_specs = [pl.BlockSpec((1, tq, kw), row), pl.BlockSpec((1, blk, kw), prev),
                pl.BlockSpec((1, blk, kw), nxt)]
    return pl.pallas_call(
        _attn_body,
        grid=(bsz, n // tq),
        in_specs=[pl.BlockSpec(memory_space=pltpu.SMEM),
                  pl.BlockSpec((1, tq, qw), row)]
                 + kv_specs + kv_specs
                 + [pl.BlockSpec((1, nc, kw), batch), pl.BlockSpec((1, nc, kw), batch),
                    pl.BlockSpec((1, tq, qw), row),
                    pl.BlockSpec(w_out.shape, lambda b, i: (0, 0)),
                    pl.BlockSpec((1, d), lambda b, i: (0, 0)),
                    pl.BlockSpec((1, 1, d), batch),
                    pl.BlockSpec((1, tq, d), row)],
        out_specs=pl.BlockSpec((1, tq, d), row),
        out_shape=jax.ShapeDtypeStruct((bsz, n, d), F32),
        scratch_shapes=[pltpu.VMEM((tq + 2 * blk, kw), BF16),
                        pltpu.VMEM((tq + 2 * blk, kw), BF16),
                        pltpu.VMEM((tq, qw), F32)],
        compiler_params=_params("parallel", "arbitrary"),
        name="window_attention_out",
    )(sink, q, k, k, k, v, v, v, kc, vc, sg, w_out, g_post.reshape(1, d), gt, x)


def _mod_vectors(c, c_ctx, w_mod, b_mod):
    bsz, d = c.shape
    rows = -(-(bsz + 1) // SUBLANES) * SUBLANES
    cvec = jnp.zeros((rows, d), F32).at[:bsz].set(c).at[bsz].set(c_ctx)
    m = _modulation(cvec, w_mod, b_mod)
    lat = [m[:bsz, k * d:(k + 1) * d].reshape(bsz, 1, d) for k in range(3)]
    cx = [m[bsz:bsz + 1, k * d:(k + 1) * d].reshape(1, 1, d) for k in range(3)]
    return lat, cx


def kernel(x, c, ctx, c_ctx, l0_w_mod, l0_b_mod, l0_g_pre, l0_g_post, l0_w_in, l0_w_conv, l0_b_conv,
           l0_w_a, l0_b_a, l0_w_x, l0_b_x, l0_lam, l0_w_f, l0_b_f, l0_w_out,
           l1_w_mod, l1_b_mod, l1_g_pre, l1_g_post, l1_w_in, l1_sink, l1_w_out):
    bsz, n, d = x.shape
    lru_w = l0_w_conv.shape[1]
    f_w = l0_w_f.shape[0] * l0_w_f.shape[1]
    assert lru_w == f_w == d

    (sh_l, sc_l, gt_l), (sh_c, sc_c, gt_c) = _mod_vectors(c, c_ctx, l0_w_mod, l0_b_mod)
    wp, wq = _fold_fourier_weights(l0_w_in, l0_w_f, 2 * lru_w)
    w_cat = jnp.concatenate([l0_w_in[:, :2 * lru_w], wp, wq, l0_w_in[:, 2 * lru_w + f_w:]],
                            axis=1).astype(BF16)
    wg = _gate_weights(l0_w_a, l0_w_x)
    w_out0 = l0_w_out.astype(BF16)

    use_fft = n % (FFT_RADIX * 32) == 0 and n >= 1024
    u_c, sg_c, p_c, q_c, sgf_c = _l0_in_proj(ctx, sc_c, sh_c, l0_g_pre, w_cat, False, False)
    u_l, sg_l, p_l, q_l, sgf_l = _l0_in_proj(x, sc_l, sh_l, l0_g_pre, w_cat, True, use_fft)

    zeros = jnp.zeros((bsz, 1, lru_w), F32)
    lru_args = (l0_w_conv, l0_b_conv, wg, l0_b_a, l0_b_x, l0_lam)
    yf_c, yb_c, hf_c, hb_c = _rglru(u_c, *lru_args, zeros, zeros)
    yf_l, yb_l, _, _ = _rglru(u_l, *lru_args, hf_c, hb_c)

    fm_c = _fourier_positions(p_c, q_c, l0_b_f, sgf_c)
    fm_l = (_fourier_positions_fft if use_fft else _fourier_positions)(p_l, q_l, l0_b_f, sgf_l)

    ctx1 = _l0_out_proj(yf_c, yb_c, sg_c, fm_c, w_out0, l0_g_post, gt_c, ctx, False)
    x1 = _l0_out_proj(yf_l, yb_l, sg_l, fm_l, w_out0, l0_g_post, gt_l, x, True)

    (sh_l, sc_l, gt_l), (sh_c, sc_c, _) = _mod_vectors(c, c_ctx, l1_w_mod, l1_b_mod)
    q_width = l1_w_out.shape[0]
    kv_width = (l1_w_in.shape[1] - 2 * q_width) // 2
    w_in1 = l1_w_in.astype(BF16)
    kc, vc = _ctx_kv_proj(ctx1, sc_c, sh_c, l1_g_pre, w_in1[:, q_width:q_width + 2 * kv_width])
    q, k, v, sg1 = _l1_in_proj(x1, sc_l, sh_l, l1_g_pre, w_in1, q_width, kv_width)
    return _attention_out(q, k, v, kc, vc, sg1, l1_sink, l1_w_out.astype(BF16), l1_g_post, gt_l, x1)
```

```python
import functools
import math

import jax
import jax.numpy as jnp
import numpy as np
from jax import lax
from jax.experimental import pallas as pl
from jax.experimental.pallas import tpu as pltpu

EPS = 1e-6
NEG_INF = -1e30
LRU_HEADS = 16
LRU_C = 8.0
CONV_W = 4
CONV_LEFT = 2
FNET_GROUPS = 8
HEAD_DIM = 64
N_KV_HEADS = 4
WINDOW = 128
ATTN_BLOCK = 128
GRID_W = 64
ROPE_BASE = 10000.0
LOG2E = math.log2(math.e)

LANES = 128
SUBLANES = 8
BF16_ROWS = 16
FFT_RADIX = 16
GATE_COLS = 256
VMEM_LIMIT = 56 * 1024 * 1024

BF16 = jnp.bfloat16
F32 = jnp.float32


def _params(*sem):
    return pltpu.CompilerParams(dimension_semantics=sem, vmem_limit_bytes=VMEM_LIMIT)


def _sigmoid(z):
    return 0.5 * jnp.tanh(0.5 * z) + 0.5


def _silu(z):
    return z * _sigmoid(z)


def _rms(v, g):
    return v * lax.rsqrt(jnp.mean(v * v, axis=-1, keepdims=True) + EPS) * g


def _row_tile(n, want):
    t = min(n, want)
    assert n % t == 0
    return t


def _mod_body(c_ref, w_ref, b_ref, o_ref):
    o_ref[...] = jnp.dot(_silu(c_ref[...]), w_ref[...], preferred_element_type=F32,
                         precision=lax.Precision.HIGHEST) + b_ref[...]


def _modulation(cvec, w_mod, b_mod):
    rows, d = cvec.shape
    width = w_mod.shape[1]
    tn = _row_tile(width, 1024)
    return pl.pallas_call(
        _mod_body,
        grid=(width // tn,),
        in_specs=[pl.BlockSpec((rows, d), lambda j: (0, 0)),
                  pl.BlockSpec((d, tn), lambda j: (0, j)),
                  pl.BlockSpec((1, tn), lambda j: (0, j))],
        out_specs=pl.BlockSpec((rows, tn), lambda j: (0, j)),
        out_shape=jax.ShapeDtypeStruct((rows, width), F32),
        compiler_params=_params("arbitrary"),
        name="modulation",
    )(cvec, w_mod, b_mod.reshape(1, width))


def _fold_body(win_ref, wf_ref, cd_ref, sd_ref, wp_ref, wq_ref):
    hi = lax.Precision.HIGHEST
    win = win_ref[...]
    wf = wf_ref[0]
    cw = jnp.dot(cd_ref[...], wf, preferred_element_type=F32, precision=hi)
    sw = jnp.dot(sd_ref[...], wf, preferred_element_type=F32, precision=hi)
    wp_ref[...] = jnp.dot(win, cw, preferred_element_type=F32, precision=hi)
    wq_ref[...] = jnp.dot(win, sw, preferred_element_type=F32, precision=hi)


def _fold_fourier_weights(w_in, w_f, col0):
    d = w_in.shape[0]
    groups, gd, _ = w_f.shape
    idx = np.arange(gd)
    ang = 2.0 * np.pi * ((idx[:, None] * idx[None, :]) % gd) / gd
    cd = jnp.asarray(np.cos(ang) / math.sqrt(gd), F32)
    sd = jnp.asarray(np.sin(ang) / math.sqrt(gd), F32)
    out = jax.ShapeDtypeStruct((d, groups * gd), F32)
    return pl.pallas_call(
        _fold_body,
        grid=(groups,),
        in_specs=[pl.BlockSpec((d, gd), lambda g: (0, col0 // gd + g)),
                  pl.BlockSpec((1, gd, gd), lambda g: (g, 0, 0)),
                  pl.BlockSpec((gd, gd), lambda g: (0, 0)),
                  pl.BlockSpec((gd, gd), lambda g: (0, 0))],
        out_specs=[pl.BlockSpec((d, gd), lambda g: (0, g)),
                   pl.BlockSpec((d, gd), lambda g: (0, g))],
        out_shape=[out, out],
        compiler_params=_params("arbitrary"),
        name="fold_fourier_weights",
    )(w_in, w_f, cd, sd)


def _l0_in_body(decimate, x_ref, sc_ref, sh_ref, g_ref, w_ref, u_ref, sg_ref, p_ref, q_ref, sgf_ref,
                *slab):
    width = u_ref.shape[-1]
    h = _rms(x_ref[0], g_ref[...]) * (1.0 + sc_ref[0]) + sh_ref[0]
    hb = h.astype(BF16)
    hb_dec = hb
    if decimate:
        slab_ref, = slab
        per = p_ref.shape[2]
        d = h.shape[-1]
        for c in range(d // LANES):
            slab_ref[c] = h[:, c * LANES:(c + 1) * LANES]
        hb_dec = jnp.concatenate(
            [jnp.concatenate([slab_ref[c, pl.ds(n1, per, stride=FFT_RADIX), :]
                              for c in range(d // LANES)], axis=1)
             for n1 in range(FFT_RADIX)], axis=0).astype(BF16)
    outs = (u_ref, sg_ref, p_ref, q_ref, sgf_ref)
    gated = (False, True, False, False, True)
    for part, (o_ref, is_gate) in enumerate(zip(outs, gated)):
        lhs = hb_dec if part in (2, 3) else hb
        z = jnp.dot(lhs, w_ref[:, part * width:(part + 1) * width], preferred_element_type=F32)
        if is_gate:
            z = _silu(z)
        o_ref[...] = z.astype(o_ref.dtype).reshape(o_ref.shape)


def _l0_in_proj(x, sc, sh, g_pre, w_cat, per_batch_mod, decimate):
    bsz, n, d = x.shape
    width = w_cat.shape[1] // 5
    t = _row_tile(n, 512)
    mod_idx = (lambda b, j: (b, 0, 0)) if per_batch_mod else (lambda b, j: (0, 0, 0))
    out = jax.ShapeDtypeStruct((bsz, n, width), BF16)
    row_spec = pl.BlockSpec((1, t, width), lambda b, j: (b, j, 0))
    if decimate:
        dec = jax.ShapeDtypeStruct((bsz, FFT_RADIX, n // FFT_RADIX, width), BF16)
        dec_spec = pl.BlockSpec((1, FFT_RADIX, t // FFT_RADIX, width), lambda b, j: (b, 0, j, 0))
        out_specs = [row_spec, row_spec, dec_spec, dec_spec, row_spec]
        out_shape = [out, out, dec, dec, out]
        scratch = [pltpu.VMEM((d // LANES, t, LANES), F32)]
    else:
        out_specs, out_shape, scratch = [row_spec] * 5, [out] * 5, []
    return pl.pallas_call(
        functools.partial(_l0_in_body, decimate),
        grid=(bsz, n // t),
        in_specs=[pl.BlockSpec((1, t, d), lambda b, j: (b, j, 0)),
                  pl.BlockSpec((1, 1, d), mod_idx),
                  pl.BlockSpec((1, 1, d), mod_idx),
                  pl.BlockSpec((1, d), lambda b, j: (0, 0)),
                  pl.BlockSpec(w_cat.shape, lambda b, j: (0, 0))],
        out_specs=out_specs,
        out_shape=out_shape,
        scratch_shapes=scratch,
        compiler_params=_params("parallel", "arbitrary"),
        name="l0_in_proj",
    )(x, sc, sh, g_pre.reshape(1, d), w_cat)


def _lru_direction(rev, scan_start, seq_first, seq_last, u_ref, up_ref, un_ref, wconv_ref,
                   bconv_ref, wg_ref, ba_ref, bx_ref, lam_ref, h0_ref, y_ref, hfin_ref,
                   ext_ref, a_ref, b_ref, carry_ref):
    t, width = a_ref.shape
    d = 1 if rev else 0

    prev = up_ref[0, BF16_ROWS - SUBLANES:, :].astype(F32)
    nxt = un_ref[0, :SUBLANES, :].astype(F32)
    ext_ref[0:SUBLANES, :] = jnp.where(seq_first, 0.0, prev)
    ext_ref[SUBLANES:SUBLANES + t, :] = u_ref[0].astype(F32)
    ext_ref[SUBLANES + t:, :] = jnp.where(seq_last, 0.0, nxt)

    z = -lam_ref[d:d + 1, :]
    c_lam = -LRU_C * (jnp.maximum(z, 0.0) + jnp.log1p(jnp.exp(-jnp.abs(z))))

    for cg in range(width // GATE_COLS):
        cols = slice(cg * GATE_COLS, (cg + 1) * GATE_COLS)
        uc = bconv_ref[:, cols] + jnp.zeros((t, GATE_COLS), F32)
        for k in range(CONV_W):
            start = SUBLANES + k - CONV_LEFT
            uc = uc + ext_ref[start:start + t, cols] * wconv_ref[k:k + 1, cols]
        zg = jnp.dot(uc.astype(BF16), wg_ref[d, cg], preferred_element_type=F32)
        r = _sigmoid(zg[:, :GATE_COLS] + ba_ref[d:d + 1, cols])
        i = _sigmoid(zg[:, GATE_COLS:] + bx_ref[d:d + 1, cols])
        log_a = c_lam[:, cols] * r
        a = jnp.exp(log_a)
        amp = jnp.sqrt(-jnp.tanh(log_a) * (1.0 + a * a))
        a_ref[:, cols] = a
        b_ref[:, cols] = amp * (i * uc)

    @pl.when(scan_start)
    def _():
        carry_ref[d:d + 1, :] = h0_ref[0]

    row = lax.broadcasted_iota(jnp.int32, (SUBLANES, width), 0)
    groups = t // SUBLANES

    def body(it, h):
        g = (groups - 1 - it) if rev else it
        rows = pl.ds(pl.multiple_of(g * SUBLANES, SUBLANES), SUBLANES)
        av = a_ref[rows, :]
        bv = b_ref[rows, :]
        for s in (1, 2, 4):
            if rev:
                shift, m = SUBLANES - s, row < SUBLANES - s
            else:
                shift, m = s, row >= s
            a_sh = pltpu.roll(av, shift, 0)
            b_sh = pltpu.roll(bv, shift, 0)
            bv = jnp.where(m, av * b_sh + bv, bv)
            av = jnp.where(m, av * a_sh, av)
        hrows = av * h + bv
        b_ref[rows, :] = hrows
        return hrows[0:1, :] if rev else hrows[SUBLANES - 1:SUBLANES, :]

    h_last = lax.fori_loop(0, groups, body, carry_ref[d:d + 1, :])
    carry_ref[d:d + 1, :] = h_last
    hfin_ref[0] = h_last
    y_ref[0] = b_ref[...].astype(y_ref.dtype)


def _lru_body(uf_ref, ufp_ref, ufn_ref, ub_ref, ubp_ref, ubn_ref, wconv_ref, bconv_ref, wg_ref,
              ba_ref, bx_ref, lam_ref, h0f_ref, h0b_ref,
              yf_ref, yb_ref, hf_ref, hb_ref, ext_ref, a_ref, b_ref, carry_ref):
    j = pl.program_id(1)
    nt = pl.num_programs(1)
    common = (wconv_ref, bconv_ref, wg_ref, ba_ref, bx_ref, lam_ref)
    scratch = (ext_ref, a_ref, b_ref, carry_ref)
    first, last = j == 0, j == nt - 1
    _lru_direction(False, first, first, last, uf_ref, ufp_ref, ufn_ref, *common,
                   h0f_ref, yf_ref, hf_ref, *scratch)
    _lru_direction(True, first, last, first, ub_ref, ubp_ref, ubn_ref, *common,
                   h0b_ref, yb_ref, hb_ref, *scratch)


def _rglru(u, w_conv, b_conv, wg, b_a, b_x, lam, h0f, h0b):
    bsz, n, width = u.shape
    t = _row_tile(n, 512)
    nt = n // t
    hb_per = t // BF16_ROWS
    last_h = n // BF16_ROWS - 1

    def main(b, j):
        return (b, j, 0)

    def prev(b, j):
        return (b, jnp.maximum(j * hb_per - 1, 0), 0)

    def nxt(b, j):
        return (b, jnp.minimum((j + 1) * hb_per, last_h), 0)

    def rev(f):
        return lambda b, j: f(b, nt - 1 - j)

    tile = (1, t, width)
    halo = (1, BF16_ROWS, width)
    const2 = lambda b, j: (0, 0)
    state = pl.BlockSpec((1, 1, width), lambda b, j: (b, 0, 0))
    y_shape = jax.ShapeDtypeStruct((bsz, n, width), BF16)
    h_shape = jax.ShapeDtypeStruct((bsz, 1, width), F32)
    return pl.pallas_call(
        _lru_body,
        grid=(bsz, nt),
        in_specs=[pl.BlockSpec(tile, main), pl.BlockSpec(halo, prev), pl.BlockSpec(halo, nxt),
                  pl.BlockSpec(tile, rev(main)), pl.BlockSpec(halo, rev(prev)),
                  pl.BlockSpec(halo, rev(nxt)),
                  pl.BlockSpec(w_conv.shape, const2),
                  pl.BlockSpec((1, width), const2),
                  pl.BlockSpec(wg.shape, lambda b, j: (0, 0, 0, 0)),
                  pl.BlockSpec(b_a.shape, const2),
                  pl.BlockSpec(b_x.shape, const2),
                  pl.BlockSpec(lam.shape, const2),
                  state, state],
        out_specs=[pl.BlockSpec(tile, main), pl.BlockSpec(tile, rev(main)), state, state],
        out_shape=[y_shape, y_shape, h_shape, h_shape],
        scratch_shapes=[pltpu.VMEM((t + 2 * SUBLANES, width), F32),
                        pltpu.VMEM((t, width), F32),
                        pltpu.VMEM((t, width), F32),
                        pltpu.VMEM((2, width), F32)],
        compiler_params=_params("parallel", "arbitrary"),
        name="rglru_scan",
    )(u, u, u, u, u, u, w_conv, b_conv.reshape(1, width), wg, b_a, b_x, lam, h0f, h0b)


def _gate_weights(w_a, w_x):
    _, heads, blk, _ = w_a.shape
    per = GATE_COLS // blk
    groups = heads // per
    eye = jnp.eye(per, dtype=w_a.dtype)

    def bd(w):
        w = w.reshape(2, groups, per, blk, blk)
        full = jnp.einsum('dgpij,pq->dgpiqj', w, eye)
        return full.reshape(2, groups, GATE_COLS, GATE_COLS)

    return jnp.concatenate([bd(w_a), bd(w_x)], axis=-1).astype(BF16)


def _dft_body(c_ref, s_ref, p_ref, q_ref, bf_ref, sgf_ref, o_ref, acc_ref):
    k = pl.program_id(2)

    @pl.when(k == 0)
    def _():
        acc_ref[...] = jnp.zeros_like(acc_ref)

    acc_ref[...] += (jnp.dot(c_ref[...], p_ref[0], preferred_element_type=F32)
                     + jnp.dot(s_ref[...], q_ref[0], preferred_element_type=F32))

    @pl.when(k == pl.num_programs(2) - 1)
    def _():
        o_ref[0] = ((acc_ref[...] + bf_ref[...]) * sgf_ref[0].astype(F32)).astype(o_ref.dtype)


def _dft_matrices(n):
    n1 = 1
    while n1 * n1 < n:
        n1 *= 2
    n2 = n // n1
    m = jnp.arange(n, dtype=jnp.int32)[None, :]
    hi = jnp.arange(n2, dtype=jnp.int32)[:, None]
    lo = jnp.arange(n1, dtype=jnp.int32)[:, None]
    ang_hi = ((hi * n1 * m) % n).astype(F32) * (2.0 * math.pi / n)
    ang_lo = ((lo * m) % n).astype(F32) * (2.0 * math.pi / n)
    ch, sh_ = jnp.cos(ang_hi)[:, None, :], jnp.sin(ang_hi)[:, None, :]
    cl, sl = jnp.cos(ang_lo)[None, :, :], jnp.sin(ang_lo)[None, :, :]
    scale = n ** -0.5
    cmat = ((ch * cl - sh_ * sl) * scale).reshape(n, n).astype(BF16)
    smat = ((sh_ * cl + ch * sl) * (-scale)).reshape(n, n).astype(BF16)
    return cmat, smat


def _fourier_positions(p, q, b_f, sgf):
    bsz, n, width = p.shape
    cmat, smat = _dft_matrices(n)
    tm = _row_tile(n, 1024)
    tk = _row_tile(n, 1024)
    return pl.pallas_call(
        _dft_body,
        grid=(bsz, n // tm, n // tk),
        in_specs=[pl.BlockSpec((tm, tk), lambda b, i, k: (i, k)),
                  pl.BlockSpec((tm, tk), lambda b, i, k: (i, k)),
                  pl.BlockSpec((1, tk, width), lambda b, i, k: (b, k, 0)),
                  pl.BlockSpec((1, tk, width), lambda b, i, k: (b, k, 0)),
                  pl.BlockSpec((1, width), lambda b, i, k: (0, 0)),
                  pl.BlockSpec((1, tm, width), lambda b, i, k: (b, i, 0))],
        out_specs=pl.BlockSpec((1, tm, width), lambda b, i, k: (b, i, 0)),
        out_shape=jax.ShapeDtypeStruct((bsz, n, width), BF16),
        scratch_shapes=[pltpu.VMEM((tm, width), F32)],
        compiler_params=_params("parallel", "parallel", "arbitrary"),
        name="fourier_positions",
    )(cmat, smat, p, q, b_f.reshape(1, width), sgf)


def _fft_real(re, im):
    root_half = math.sqrt(0.5)

    def rec(re, im, want_im):
        n = len(re)
        if n == 1:
            return list(re), list(im)
        er, ei = rec(re[0::2], im[0::2], True)
        xr, xi = rec(re[1::2], im[1::2], True)
        out_r, out_i = [None] * n, [None] * n
        for k in range(n // 2):
            if k == 0:
                tr, ti = xr[k], xi[k]
            elif 4 * k == n:
                tr, ti = xi[k], None
            elif 8 * k == n:
                tr = (xr[k] + xi[k]) * root_half
                ti = (xi[k] - xr[k]) * root_half if want_im else None
            elif 8 * k == 3 * n:
                tr = (xi[k] - xr[k]) * root_half
                ti = (xr[k] + xi[k]) * (-root_half) if want_im else None
            else:
                c, s = math.cos(2 * math.pi * k / n), math.sin(2 * math.pi * k / n)
                tr = xr[k] * c + xi[k] * s
                ti = xi[k] * c - xr[k] * s if want_im else None
            out_r[k] = er[k] + tr
            out_r[k + n // 2] = er[k] - tr
            if want_im:
                if 4 * k == n:
                    out_i[k] = ei[k] - xr[k]
                    out_i[k + n // 2] = ei[k] + xr[k]
                else:
                    out_i[k] = ei[k] + ti
                    out_i[k + n // 2] = ei[k] - ti
        return out_r, out_i

    return rec(list(re), list(im), False)[0]


def _fft_body(p_ref, q_ref, wp_ref, wq_ref, bf_ref, sgf_ref, o_ref, yre_ref, yim_ref):
    radix, half, tc = yre_ref.shape
    for n1 in range(radix):
        y = (jnp.dot(wp_ref[n1], p_ref[0, n1], preferred_element_type=F32)
             + jnp.dot(wq_ref[n1], q_ref[0, n1], preferred_element_type=F32))
        yre_ref[n1] = y[:half]
        yim_ref[n1] = y[half:]

    rows = BF16_ROWS

    def chunk(it, carry):
        r0 = pl.multiple_of(it * rows, rows)
        for lt in range(tc // LANES):
            ls = slice(lt * LANES, (lt + 1) * LANES)
            re = [yre_ref[n1, pl.ds(r0, rows), ls] for n1 in range(radix)]
            im = [yim_ref[n1, pl.ds(r0, rows), ls] for n1 in range(radix)]
            out = _fft_real(re, im)
            for k1 in range(radix):
                gate = sgf_ref[0, k1, pl.ds(r0, rows), ls].astype(F32)
                o_ref[0, k1, pl.ds(r0, rows), ls] = ((out[k1] + bf_ref[:, ls]) * gate).astype(o_ref.dtype)
        return carry

    lax.fori_loop(0, half // rows, chunk, 0)


def _fft_matrices(n):
    half = n // FFT_RADIX
    k2 = jnp.arange(half, dtype=jnp.int32)[:, None]
    n1 = jnp.arange(FFT_RADIX, dtype=jnp.int32)[None, :]
    n2 = jnp.arange(half, dtype=jnp.int32)[None, :]
    ang1 = ((k2 * n1) % n).astype(F32) * (2.0 * math.pi / n)
    ang2 = ((k2 * n2) % half).astype(F32) * (2.0 * math.pi / half)
    c1, s1 = jnp.cos(ang1).T[:, :, None], jnp.sin(ang1).T[:, :, None]
    c2, s2 = jnp.cos(ang2)[None], jnp.sin(ang2)[None]
    scale = n ** -0.5
    cm = (c1 * c2 - s1 * s2) * scale
    sm = (s1 * c2 + c1 * s2) * scale
    wp = jnp.concatenate([cm, -sm], axis=1).astype(BF16)
    wq = jnp.concatenate([-sm, -cm], axis=1).astype(BF16)
    return wp, wq


def _fourier_positions_fft(pd, qd, b_f, sgf):
    bsz, radix, half, width = pd.shape
    n = radix * half
    wp, wq = _fft_matrices(n)
    tc = 2 * LANES
    data = pl.BlockSpec((1, radix, half, tc), lambda b, c: (b, 0, 0, c))
    mats = pl.BlockSpec((radix, 2 * half, half), lambda b, c: (0, 0, 0))
    out = pl.pallas_call(
        _fft_body,
        grid=(bsz, width // tc),
        in_specs=[data, data, mats, mats,
                  pl.BlockSpec((1, tc), lambda b, c: (0, c)),
                  data],
        out_specs=data,
        out_shape=jax.ShapeDtypeStruct((bsz, radix, half, width), BF16),
        scratch_shapes=[pltpu.VMEM((radix, half, tc), F32), pltpu.VMEM((radix, half, tc), F32)],
        compiler_params=_params("parallel", "arbitrary"),
        name="fourier_positions_fft",
    )(pd, qd, wp, wq, b_f.reshape(1, width), sgf.reshape(bsz, radix, half, width))
    return out.reshape(bsz, n, width)


def _l0_out_body(yf_ref, yb_ref, sg_ref, fm_ref, w_ref, g_ref, gt_ref, x_ref, o_ref):
    width = yf_ref.shape[-1]
    r = (yf_ref[0].astype(F32) + yb_ref[0].astype(F32)) * sg_ref[0].astype(F32)
    y = (jnp.dot(r.astype(BF16), w_ref[:width, :], preferred_element_type=F32)
         + jnp.dot(fm_ref[0], w_ref[width:, :], preferred_element_type=F32))
    o_ref[0] = x_ref[0] + gt_ref[0] * _rms(y, g_ref[...])


def _l0_out_proj(yf, yb, sg, fmix, w_out, g_post, gt, x, per_batch_mod):
    bsz, n, width = yf.shape
    d = x.shape[-1]
    t = _row_tile(n, 512)
    mod_idx = (lambda b, j: (b, 0, 0)) if per_batch_mod else (lambda b, j: (0, 0, 0))
    row = lambda b, j: (b, j, 0)
    return pl.pallas_call(
        _l0_out_body,
        grid=(bsz, n // t),
        in_specs=[pl.BlockSpec((1, t, width), row)] * 4
                 + [pl.BlockSpec(w_out.shape, lambda b, j: (0, 0)),
                    pl.BlockSpec((1, d), lambda b, j: (0, 0)),
                    pl.BlockSpec((1, 1, d), mod_idx),
                    pl.BlockSpec((1, t, d), row)],
        out_specs=pl.BlockSpec((1, t, d), row),
        out_shape=jax.ShapeDtypeStruct((bsz, n, d), F32),
        compiler_params=_params("parallel", "arbitrary"),
        name="l0_out_proj",
    )(yf, yb, sg, fmix, w_out, g_post.reshape(1, d), gt, x)


def _rope(z, cos, sin_signed):
    quarter = HEAD_DIM // 4
    lane = lax.broadcasted_iota(jnp.int32, (z.shape[0], LANES), 1)
    first = (lane % (2 * quarter)) < quarter
    chunks = []
    for c in range(z.shape[1] // LANES):
        zc = z[:, c * LANES:(c + 1) * LANES]
        swapped = jnp.where(first, pltpu.roll(zc, LANES - quarter, 1), pltpu.roll(zc, quarter, 1))
        chunks.append(zc * cos + swapped * sin_signed)
    return jnp.concatenate(chunks, axis=1)


def _l1_in_body(x_ref, sc_ref, sh_ref, g_ref, w_ref, cos_ref, sin_ref, q_ref, k_ref, v_ref, sg_ref):
    qw, kw = q_ref.shape[-1], k_ref.shape[-1]
    h = _rms(x_ref[0], g_ref[...]) * (1.0 + sc_ref[0]) + sh_ref[0]
    hb = h.astype(BF16)
    cos, sin = cos_ref[...], sin_ref[...]
    zq = jnp.dot(hb, w_ref[:, :qw], preferred_element_type=F32)
    q_ref[0] = (_rope(zq, cos, sin) * (HEAD_DIM ** -0.5 * LOG2E)).astype(q_ref.dtype)
    zk = jnp.dot(hb, w_ref[:, qw:qw + kw], preferred_element_type=F32)
    k_ref[0] = _rope(zk, cos, sin).astype(k_ref.dtype)
    zv = jnp.dot(hb, w_ref[:, qw + kw:qw + 2 * kw], preferred_element_type=F32)
    v_ref[0] = zv.astype(v_ref.dtype)
    zg = jnp.dot(hb, w_ref[:, qw + 2 * kw:], preferred_element_type=F32)
    sg_ref[0] = _silu(zg).astype(sg_ref.dtype)


def _rope_tables(n):
    t = jnp.arange(n)
    row = (t // GRID_W).astype(F32)
    col = (t % GRID_W).astype(F32)
    half = HEAD_DIM // 2
    freqs = ROPE_BASE ** (-jnp.arange(0, half, 2, dtype=F32) / half)
    ar = row[:, None] * freqs[None, :]
    ac = col[:, None] * freqs[None, :]
    cos = jnp.concatenate([jnp.cos(ar), jnp.cos(ar), jnp.cos(ac), jnp.cos(ac)], axis=1)
    sin = jnp.concatenate([-jnp.sin(ar), jnp.sin(ar), -jnp.sin(ac), jnp.sin(ac)], axis=1)
    reps = LANES // HEAD_DIM
    return jnp.tile(cos, (1, reps)), jnp.tile(sin, (1, reps))


def _l1_in_proj(x, sc, sh, g_pre, w_in, q_width, kv_width):
    bsz, n, d = x.shape
    t = _row_tile(n, 512)
    cos, sin = _rope_tables(n)
    row = lambda b, j: (b, j, 0)
    mod_idx = lambda b, j: (b, 0, 0)
    return pl.pallas_call(
        _l1_in_body,
        grid=(bsz, n // t),
        in_specs=[pl.BlockSpec((1, t, d), row),
                  pl.BlockSpec((1, 1, d), mod_idx),
                  pl.BlockSpec((1, 1, d), mod_idx),
                  pl.BlockSpec((1, d), lambda b, j: (0, 0)),
                  pl.BlockSpec(w_in.shape, lambda b, j: (0, 0)),
                  pl.BlockSpec((t, LANES), lambda b, j: (j, 0)),
                  pl.BlockSpec((t, LANES), lambda b, j: (j, 0))],
        out_specs=[pl.BlockSpec((1, t, q_width), row),
                   pl.BlockSpec((1, t, kv_width), row),
                   pl.BlockSpec((1, t, kv_width), row),
                   pl.BlockSpec((1, t, q_width), row)],
        out_shape=[jax.ShapeDtypeStruct((bsz, n, q_width), BF16),
                   jax.ShapeDtypeStruct((bsz, n, kv_width), BF16),
                   jax.ShapeDtypeStruct((bsz, n, kv_width), BF16),
                   jax.ShapeDtypeStruct((bsz, n, q_width), BF16)],
        compiler_params=_params("parallel", "arbitrary"),
        name="l1_in_proj",
    )(x, sc, sh, g_pre.reshape(1, d), w_in, cos, sin)


def _ctx_kv_body(x_ref, sc_ref, sh_ref, g_ref, w_ref, k_ref, v_ref):
    kw = k_ref.shape[-1]
    h = _rms(x_ref[0], g_ref[...]) * (1.0 + sc_ref[0]) + sh_ref[0]
    z = jnp.dot(h.astype(BF16), w_ref[...], preferred_element_type=F32)
    k_ref[0] = z[:, :kw].astype(k_ref.dtype)
    v_ref[0] = z[:, kw:].astype(v_ref.dtype)


def _ctx_kv_proj(ctx, sc, sh, g_pre, w_kv):
    bsz, n, d = ctx.shape
    kw = w_kv.shape[1] // 2
    row = lambda b: (b, 0, 0)
    shared = lambda b: (0, 0, 0)
    out = jax.ShapeDtypeStruct((bsz, n, kw), BF16)
    return pl.pallas_call(
        _ctx_kv_body,
        grid=(bsz,),
        in_specs=[pl.BlockSpec((1, n, d), row),
                  pl.BlockSpec((1, 1, d), shared),
                  pl.BlockSpec((1, 1, d), shared),
                  pl.BlockSpec((1, d), lambda b: (0, 0)),
                  pl.BlockSpec(w_kv.shape, lambda b: (0, 0))],
        out_specs=[pl.BlockSpec((1, n, kw), row)] * 2,
        out_shape=[out, out],
        compiler_params=_params("arbitrary"),
        name="ctx_kv_proj",
    )(ctx, sc, sh, g_pre.reshape(1, d), w_kv)


def _attn_body(sink_ref, q_ref, km_ref, kp_ref, kn_ref, vm_ref, vp_ref, vn_ref, kc_ref, vc_ref,
               sg_ref, w_ref, g_ref, gt_ref, x_ref, o_ref, kf_ref, vf_ref, att_ref):
    i = pl.program_id(1)
    ni = pl.num_programs(1)
    tq = q_ref.shape[1]
    blk = ATTN_BLOCK
    per_step = tq // blk
    group = q_ref.shape[-1] // HEAD_DIM // N_KV_HEADS

    kf_ref[0:blk, :] = kp_ref[0]
    kf_ref[blk:blk + tq, :] = km_ref[0]
    kf_ref[blk + tq:, :] = kn_ref[0]
    vf_ref[0:blk, :] = vp_ref[0]
    vf_ref[blk:blk + tq, :] = vm_ref[0]
    vf_ref[blk + tq:, :] = vn_ref[0]

    ri = lax.broadcasted_iota(jnp.int32, (blk, blk), 0)
    cj = lax.broadcasted_iota(jnp.int32, (blk, blk), 1)
    tri_prev = jnp.where(cj >= ri, 0.0, NEG_INF)
    tri_next = jnp.where(cj <= ri, 0.0, NEG_INF)
    nt_dims = (((1,), (1,)), ((), ()))

    for qb in range(per_step):
        r0 = qb * blk
        prev_bias = tri_prev + jnp.where((i * per_step + qb) > 0, 0.0, NEG_INF)
        next_bias = tri_next + jnp.where((i * per_step + qb) < ni * per_step - 1, 0.0, NEG_INF)
        for kh in range(N_KV_HEADS):
            hs = slice(kh * HEAD_DIM, (kh + 1) * HEAD_DIM)
            q4 = jnp.concatenate(
                [q_ref[0, r0:r0 + blk, (kh * group + g) * HEAD_DIM:(kh * group + g + 1) * HEAD_DIM]
                 for g in range(group)], axis=0)
            s_lat = lax.dot_general(q4, kf_ref[r0:r0 + 3 * blk, hs], nt_dims,
                                    preferred_element_type=F32)
            s_ctx = lax.dot_general(q4, kc_ref[0, :, hs], nt_dims, preferred_element_type=F32)
            v_lat = vf_ref[r0:r0 + 3 * blk, hs]
            v_ctx = vc_ref[0, :, hs]
            outs = []
            for g in range(group):
                rows = slice(g * blk, (g + 1) * blk)
                sink = sink_ref[kh * group + g]
                cols = [s_lat[rows, :blk] + prev_bias, s_lat[rows, blk:2 * blk],
                        s_lat[rows, 2 * blk:] + next_bias]
                cols += [s_ctx[rows, c * blk:(c + 1) * blk] for c in range(s_ctx.shape[1] // blk)]
                mx = functools.reduce(jnp.maximum, cols)
                m = jnp.maximum(jnp.max(mx, axis=-1, keepdims=True), sink)
                e = [jnp.exp2(c - m) for c in cols]
                denom = (jnp.sum(functools.reduce(jnp.add, e), axis=-1, keepdims=True)
                         + jnp.exp2(sink - m))
                og = (jnp.dot(jnp.concatenate(e[:3], axis=1).astype(BF16), v_lat,
                              preferred_element_type=F32)
                      + jnp.dot(jnp.concatenate(e[3:], axis=1).astype(BF16), v_ctx,
                                preferred_element_type=F32))
                outs.append(og / denom)
            att_ref[r0:r0 + blk, kh * group * HEAD_DIM:(kh + 1) * group * HEAD_DIM] = (
                jnp.concatenate(outs, axis=1))

    gated = (att_ref[...] * sg_ref[0].astype(F32)).astype(BF16)
    y = jnp.dot(gated, w_ref[...], preferred_element_type=F32)
    o_ref[0] = x_ref[0] + gt_ref[0] * _rms(y, g_ref[...])


def _attention_out(q, k, v, kc, vc, sg, sink, w_out, g_post, gt, x):
    bsz, n, qw = q.shape
    kw = k.shape[-1]
    nc = kc.shape[1]
    d = x.shape[-1]
    blk = ATTN_BLOCK
    tq = _row_tile(n, 512)
    per = tq // blk
    nb = n // blk
    row = lambda b, i: (b, i, 0)
    prev = lambda b, i: (b, jnp.maximum(i * per - 1, 0), 0)
    nxt = lambda b, i: (b, jnp.minimum((i + 1) * per, nb - 1), 0)
    batch = lambda b, i: (b, 0, 0)
    kv_specs = [pl.BlockSpec((1, tq, kw), row), pl.BlockSpec((1, blk, kw), prev),
                pl.BlockSpec((1, blk, kw), nxt)]
    return pl.pallas_call(
        _attn_body,
        grid=(bsz, n // tq),
        in_specs=[pl.BlockSpec(memory_space=pltpu.SMEM),
                  pl.BlockSpec((1, tq, qw), row)]
                 + kv_specs + kv_specs
                 + [pl.BlockSpec((1, nc, kw), batch), pl.BlockSpec((1, nc, kw), batch),
                    pl.BlockSpec((1, tq, qw), row),
                    pl.BlockSpec(w_out.shape, lambda b, i: (0, 0)),
                    pl.BlockSpec((1, d), lambda b, i: (0, 0)),
                    pl.BlockSpec((1, 1, d), batch),
                    pl.BlockSpec((1, tq, d), row)],
        out_specs=pl.BlockSpec((1, tq, d), row),
        out_shape=jax.ShapeDtypeStruct((bsz, n, d), F32),
        scratch_shapes=[pltpu.VMEM((tq + 2 * blk, kw), BF16),
                        pltpu.VMEM((tq + 2 * blk, kw), BF16),
                        pltpu.VMEM((tq, qw), F32)],
        compiler_params=_params("parallel", "arbitrary"),
        name="window_attention_out",
    )(sink, q, k, k, k, v, v, v, kc, vc, sg, w_out, g_post.reshape(1, d), gt, x)


def _mod_vectors(c, c_ctx, w_mod, b_mod):
    bsz, d = c.shape
    rows = -(-(bsz + 1) // SUBLANES) * SUBLANES
    cvec = jnp.zeros((rows, d), F32).at[:bsz].set(c).at[bsz].set(c_ctx)
    m = _modulation(cvec, w_mod, b_mod)
    lat = [m[:bsz, k * d:(k + 1) * d].reshape(bsz, 1, d) for k in range(3)]
    cx = [m[bsz:bsz + 1, k * d:(k + 1) * d].reshape(1, 1, d) for k in range(3)]
    return lat, cx


def kernel(x, c, ctx, c_ctx, l0_w_mod, l0_b_mod, l0_g_pre, l0_g_post, l0_w_in, l0_w_conv, l0_b_conv,
           l0_w_a, l0_b_a, l0_w_x, l0_b_x, l0_lam, l0_w_f, l0_b_f, l0_w_out,
           l1_w_mod, l1_b_mod, l1_g_pre, l1_g_post, l1_w_in, l1_sink, l1_w_out):
    bsz, n, d = x.shape
    lru_w = l0_w_conv.shape[1]
    f_w = l0_w_f.shape[0] * l0_w_f.shape[1]
    assert lru_w == f_w == d

    (sh_l, sc_l, gt_l), (sh_c, sc_c, gt_c) = _mod_vectors(c, c_ctx, l0_w_mod, l0_b_mod)
    wp, wq = _fold_fourier_weights(l0_w_in, l0_w_f, 2 * lru_w)
    w_cat = jnp.concatenate([l0_w_in[:, :2 * lru_w], wp, wq, l0_w_in[:, 2 * lru_w + f_w:]],
                            axis=1).astype(BF16)
    wg = _gate_weights(l0_w_a, l0_w_x)
    w_out0 = l0_w_out.astype(BF16)

    use_fft = n % (FFT_RADIX * 32) == 0 and n >= 1024
    u_c, sg_c, p_c, q_c, sgf_c = _l0_in_proj(ctx, sc_c, sh_c, l0_g_pre, w_cat, False, False)
    u_l, sg_l, p_l, q_l, sgf_l = _l0_in_proj(x, sc_l, sh_l, l0_g_pre, w_cat, True, use_fft)

    zeros = jnp.zeros((bsz, 1, lru_w), F32)
    lru_args = (l0_w_conv, l0_b_conv, wg, l0_b_a, l0_b_x, l0_lam)
    yf_c, yb_c, hf_c, hb_c = _rglru(u_c, *lru_args, zeros, zeros)
    yf_l, yb_l, _, _ = _rglru(u_l, *lru_args, hf_c, hb_c)

    fm_c = _fourier_positions(p_c, q_c, l0_b_f, sgf_c)
    fm_l = (_fourier_positions_fft if use_fft else _fourier_positions)(p_l, q_l, l0_b_f, sgf_l)

    ctx1 = _l0_out_proj(yf_c, yb_c, sg_c, fm_c, w_out0, l0_g_post, gt_c, ctx, False)
    x1 = _l0_out_proj(yf_l, yb_l, sg_l, fm_l, w_out0, l0_g_post, gt_l, x, True)

    (sh_l, sc_l, gt_l), (sh_c, sc_c, _) = _mod_vectors(c, c_ctx, l1_w_mod, l1_b_mod)
    q_width = l1_w_out.shape[0]
    kv_width = (l1_w_in.shape[1] - 2 * q_width) // 2
    w_in1 = l1_w_in.astype(BF16)
    kc, vc = _ctx_kv_proj(ctx1, sc_c, sh_c, l1_g_pre, w_in1[:, q_width:q_width + 2 * kv_width])
    q, k, v, sg1 = _l1_in_proj(x1, sc_l, sh_l, l1_g_pre, w_in1, q_width, kv_width)
    return _attention_out(q, k, v, kc, vc, sg1, l1_sink * LOG2E, l1_w_out.astype(BF16), l1_g_post,
                          gt_l, x1)
```

```python
import functools
import math

import jax
import jax.numpy as jnp
import numpy as np
from jax import lax
from jax.experimental import pallas as pl
from jax.experimental.pallas import tpu as pltpu

EPS = 1e-6
NEG_INF = -1e30
LRU_HEADS = 16
LRU_C = 8.0
CONV_W = 4
CONV_LEFT = 2
FNET_GROUPS = 8
HEAD_DIM = 64
N_KV_HEADS = 4
WINDOW = 128
ATTN_BLOCK = 128
GRID_W = 64
ROPE_BASE = 10000.0
LOG2E = math.log2(math.e)

LANES = 128
SUBLANES = 8
BF16_ROWS = 16
FFT_RADIX = 16
GATE_COLS = 256
VMEM_LIMIT = 56 * 1024 * 1024

BF16 = jnp.bfloat16
F32 = jnp.float32


def _params(*sem):
    return pltpu.CompilerParams(dimension_semantics=sem, vmem_limit_bytes=VMEM_LIMIT)


def _sigmoid(z):
    return 0.5 * jnp.tanh(0.5 * z) + 0.5


def _silu(z):
    return z * _sigmoid(z)


def _rms(v, g):
    return v * lax.rsqrt(jnp.mean(v * v, axis=-1, keepdims=True) + EPS) * g


def _row_tile(n, want):
    t = min(n, want)
    assert n % t == 0
    return t


def _mod_body(c_ref, w_ref, b_ref, o_ref):
    o_ref[...] = jnp.dot(_silu(c_ref[...]), w_ref[...], preferred_element_type=F32,
                         precision=lax.Precision.HIGHEST) + b_ref[...]


def _modulation(cvec, w_mod, b_mod):
    rows, d = cvec.shape
    width = w_mod.shape[1]
    tn = _row_tile(width, 1024)
    return pl.pallas_call(
        _mod_body,
        grid=(width // tn,),
        in_specs=[pl.BlockSpec((rows, d), lambda j: (0, 0)),
                  pl.BlockSpec((d, tn), lambda j: (0, j)),
                  pl.BlockSpec((1, tn), lambda j: (0, j))],
        out_specs=pl.BlockSpec((rows, tn), lambda j: (0, j)),
        out_shape=jax.ShapeDtypeStruct((rows, width), F32),
        compiler_params=_params("arbitrary"),
        name="modulation",
    )(cvec, w_mod, b_mod.reshape(1, width))


def _fold_body(win_ref, wf_ref, cd_ref, sd_ref, wp_ref, wq_ref):
    hi = lax.Precision.HIGHEST
    win = win_ref[...]
    wf = wf_ref[0]
    cw = jnp.dot(cd_ref[...], wf, preferred_element_type=F32, precision=hi)
    sw = jnp.dot(sd_ref[...], wf, preferred_element_type=F32, precision=hi)
    wp_ref[...] = jnp.dot(win, cw, preferred_element_type=F32, precision=hi)
    wq_ref[...] = jnp.dot(win, sw, preferred_element_type=F32, precision=hi)


def _fold_fourier_weights(w_in, w_f, col0):
    d = w_in.shape[0]
    groups, gd, _ = w_f.shape
    idx = np.arange(gd)
    ang = 2.0 * np.pi * ((idx[:, None] * idx[None, :]) % gd) / gd
    cd = jnp.asarray(np.cos(ang) / math.sqrt(gd), F32)
    sd = jnp.asarray(np.sin(ang) / math.sqrt(gd), F32)
    out = jax.ShapeDtypeStruct((d, groups * gd), F32)
    return pl.pallas_call(
        _fold_body,
        grid=(groups,),
        in_specs=[pl.BlockSpec((d, gd), lambda g: (0, col0 // gd + g)),
                  pl.BlockSpec((1, gd, gd), lambda g: (g, 0, 0)),
                  pl.BlockSpec((gd, gd), lambda g: (0, 0)),
                  pl.BlockSpec((gd, gd), lambda g: (0, 0))],
        out_specs=[pl.BlockSpec((d, gd), lambda g: (0, g)),
                   pl.BlockSpec((d, gd), lambda g: (0, g))],
        out_shape=[out, out],
        compiler_params=_params("arbitrary"),
        name="fold_fourier_weights",
    )(w_in, w_f, cd, sd)


def _l0_in_body(decimate, x_ref, xp_ref, xn_ref, sc_ref, sh_ref, g_ref, w_ref, wconv_ref, bconv_ref,
                u_ref, sg_ref, p_ref, q_ref, sgf_ref, *slab):
    j = pl.program_id(1)
    width = u_ref.shape[-1]
    t = x_ref.shape[1]

    def modulated(v):
        return _rms(v, g_ref[...]) * (1.0 + sc_ref[0]) + sh_ref[0]

    h = modulated(x_ref[0])
    hb = h.astype(BF16)

    h_prev = modulated(xp_ref[0]) * jnp.where(j > 0, 1.0, 0.0)
    h_next = modulated(xn_ref[0]) * jnp.where(j < pl.num_programs(1) - 1, 1.0, 0.0)
    h_ext = jnp.concatenate([h_prev, h, h_next], axis=0).astype(BF16)
    z_ext = jnp.dot(h_ext, w_ref[:, :width], preferred_element_type=F32)
    uc = bconv_ref[...] + jnp.zeros((t, width), F32)
    for k in range(CONV_W):
        shift = (CONV_LEFT - k) % (t + 2 * SUBLANES)
        tap = z_ext if shift == 0 else pltpu.roll(z_ext, shift, 0)
        uc = uc + tap[SUBLANES:SUBLANES + t] * wconv_ref[k:k + 1, :]
    u_ref[0] = uc.astype(u_ref.dtype)

    hb_dec = hb
    if decimate:
        slab_ref, = slab
        per = p_ref.shape[2]
        d = h.shape[-1]
        for c in range(d // LANES):
            slab_ref[c] = h[:, c * LANES:(c + 1) * LANES]
        hb_dec = jnp.concatenate(
            [jnp.concatenate([slab_ref[c, pl.ds(n1, per, stride=FFT_RADIX), :]
                              for c in range(d // LANES)], axis=1)
             for n1 in range(FFT_RADIX)], axis=0).astype(BF16)
    outs = (None, sg_ref, p_ref, q_ref, sgf_ref)
    gated = (False, True, False, False, True)
    for part, (o_ref, is_gate) in enumerate(zip(outs, gated)):
        if o_ref is None:
            continue
        lhs = hb_dec if part in (2, 3) else hb
        z = jnp.dot(lhs, w_ref[:, part * width:(part + 1) * width], preferred_element_type=F32)
        if is_gate:
            z = _silu(z)
        o_ref[...] = z.astype(o_ref.dtype).reshape(o_ref.shape)


def _l0_in_proj(x, sc, sh, g_pre, w_cat, w_conv, b_conv, per_batch_mod, decimate):
    bsz, n, d = x.shape
    width = w_cat.shape[1] // 5
    t = _row_tile(n, 512)
    per8 = t // SUBLANES
    last8 = n // SUBLANES - 1
    mod_idx = (lambda b, j: (b, 0, 0)) if per_batch_mod else (lambda b, j: (0, 0, 0))
    out = jax.ShapeDtypeStruct((bsz, n, width), BF16)
    row_spec = pl.BlockSpec((1, t, width), lambda b, j: (b, j, 0))
    if decimate:
        dec = jax.ShapeDtypeStruct((bsz, FFT_RADIX, n // FFT_RADIX, width), BF16)
        dec_spec = pl.BlockSpec((1, FFT_RADIX, t // FFT_RADIX, width), lambda b, j: (b, 0, j, 0))
        out_specs = [row_spec, row_spec, dec_spec, dec_spec, row_spec]
        out_shape = [out, out, dec, dec, out]
        scratch = [pltpu.VMEM((d // LANES, t, LANES), F32)]
    else:
        out_specs, out_shape, scratch = [row_spec] * 5, [out] * 5, []
    return pl.pallas_call(
        functools.partial(_l0_in_body, decimate),
        grid=(bsz, n // t),
        in_specs=[pl.BlockSpec((1, t, d), lambda b, j: (b, j, 0)),
                  pl.BlockSpec((1, SUBLANES, d), lambda b, j: (b, jnp.maximum(j * per8 - 1, 0), 0)),
                  pl.BlockSpec((1, SUBLANES, d),
                               lambda b, j: (b, jnp.minimum((j + 1) * per8, last8), 0)),
                  pl.BlockSpec((1, 1, d), mod_idx),
                  pl.BlockSpec((1, 1, d), mod_idx),
                  pl.BlockSpec((1, d), lambda b, j: (0, 0)),
                  pl.BlockSpec(w_cat.shape, lambda b, j: (0, 0)),
                  pl.BlockSpec(w_conv.shape, lambda b, j: (0, 0)),
                  pl.BlockSpec((1, width), lambda b, j: (0, 0))],
        out_specs=out_specs,
        out_shape=out_shape,
        scratch_shapes=scratch,
        compiler_params=_params("parallel", "arbitrary"),
        name="l0_in_proj",
    )(x, x, x, sc, sh, g_pre.reshape(1, d), w_cat, w_conv, b_conv.reshape(1, width))


def _lru_direction(rev, scan_start, u_ref, wg_ref, ba_ref, bx_ref, lam_ref, h0_ref, y_ref, hfin_ref,
                   a_ref, b_ref, carry_ref):
    t, width = a_ref.shape
    d = 1 if rev else 0

    z = -lam_ref[d:d + 1, :]
    c_half = (-0.5 * LRU_C) * (jnp.maximum(z, 0.0) + jnp.log1p(jnp.exp(-jnp.abs(z))))
    ba_half = 0.5 * ba_ref[d:d + 1, :]
    bx_half = 0.5 * bx_ref[d:d + 1, :]

    for cg in range(width // GATE_COLS):
        cols = slice(cg * GATE_COLS, (cg + 1) * GATE_COLS)
        ub = u_ref[0, :, cols]
        zg = jnp.dot(ub, wg_ref[d, cg], preferred_element_type=F32)
        ta = jnp.tanh(zg[:, :GATE_COLS] + ba_half[:, cols])
        ti = jnp.tanh(zg[:, GATE_COLS:] + bx_half[:, cols])
        log_a = c_half[:, cols] * ta + c_half[:, cols]
        a = jnp.exp(log_a)
        v = -jnp.tanh(log_a) * (1.0 + a * a)
        amp = jnp.where(v > 0.0, v * lax.rsqrt(v), 0.0)
        u_half = 0.5 * ub.astype(F32)
        a_ref[:, cols] = a
        b_ref[:, cols] = amp * (u_half * ti + u_half)

    @pl.when(scan_start)
    def _():
        carry_ref[d:d + 1, :] = h0_ref[0]

    row = lax.broadcasted_iota(jnp.int32, (SUBLANES, width), 0)
    groups = t // SUBLANES

    def body(it, h):
        g = (groups - 1 - it) if rev else it
        rows = pl.ds(pl.multiple_of(g * SUBLANES, SUBLANES), SUBLANES)
        av = a_ref[rows, :]
        bv = b_ref[rows, :]
        for s in (1, 2, 4):
            if rev:
                shift, m = SUBLANES - s, row < SUBLANES - s
            else:
                shift, m = s, row >= s
            a_sh = pltpu.roll(av, shift, 0)
            b_sh = pltpu.roll(bv, shift, 0)
            bv = jnp.where(m, av * b_sh + bv, bv)
            av = jnp.where(m, av * a_sh, av)
        hrows = av * h + bv
        b_ref[rows, :] = hrows
        return hrows[0:1, :] if rev else hrows[SUBLANES - 1:SUBLANES, :]

    h_last = lax.fori_loop(0, groups, body, carry_ref[d:d + 1, :], unroll=4)
    carry_ref[d:d + 1, :] = h_last
    hfin_ref[0] = h_last
    y_ref[0] = b_ref[...].astype(y_ref.dtype)


def _lru_body(uf_ref, ub_ref, wg_ref, ba_ref, bx_ref, lam_ref, h0f_ref, h0b_ref,
              yf_ref, yb_ref, hf_ref, hb_ref, a_ref, b_ref, carry_ref):
    first = pl.program_id(1) == 0
    common = (wg_ref, ba_ref, bx_ref, lam_ref)
    scratch = (a_ref, b_ref, carry_ref)
    _lru_direction(False, first, uf_ref, *common, h0f_ref, yf_ref, hf_ref, *scratch)
    _lru_direction(True, first, ub_ref, *common, h0b_ref, yb_ref, hb_ref, *scratch)


def _rglru(u, wg, b_a, b_x, lam, h0f, h0b):
    bsz, n, width = u.shape
    t = _row_tile(n, 512)
    nt = n // t
    fwd = lambda b, j: (b, j, 0)
    bwd = lambda b, j: (b, nt - 1 - j, 0)
    tile = (1, t, width)
    const2 = lambda b, j: (0, 0)
    state = pl.BlockSpec((1, 1, width), lambda b, j: (b, 0, 0))
    y_shape = jax.ShapeDtypeStruct((bsz, n, width), BF16)
    h_shape = jax.ShapeDtypeStruct((bsz, 1, width), F32)
    return pl.pallas_call(
        _lru_body,
        grid=(bsz, nt),
        in_specs=[pl.BlockSpec(tile, fwd), pl.BlockSpec(tile, bwd),
                  pl.BlockSpec(wg.shape, lambda b, j: (0, 0, 0, 0)),
                  pl.BlockSpec(b_a.shape, const2),
                  pl.BlockSpec(b_x.shape, const2),
                  pl.BlockSpec(lam.shape, const2),
                  state, state],
        out_specs=[pl.BlockSpec(tile, fwd), pl.BlockSpec(tile, bwd), state, state],
        out_shape=[y_shape, y_shape, h_shape, h_shape],
        scratch_shapes=[pltpu.VMEM((t, width), F32),
                        pltpu.VMEM((t, width), F32),
                        pltpu.VMEM((2, width), F32)],
        compiler_params=_params("parallel", "arbitrary"),
        name="rglru_scan",
    )(u, u, wg, b_a, b_x, lam, h0f, h0b)


def _gate_weights(w_a, w_x):
    _, heads, blk, _ = w_a.shape
    per = GATE_COLS // blk
    groups = heads // per
    eye = jnp.eye(per, dtype=w_a.dtype)

    def bd(w):
        w = w.reshape(2, groups, per, blk, blk)
        full = jnp.einsum('dgpij,pq->dgpiqj', w, eye)
        return full.reshape(2, groups, GATE_COLS, GATE_COLS)

    return (0.5 * jnp.concatenate([bd(w_a), bd(w_x)], axis=-1)).astype(BF16)


def _dft_body(c_ref, s_ref, p_ref, q_ref, bf_ref, sgf_ref, o_ref, acc_ref):
    k = pl.program_id(2)

    @pl.when(k == 0)
    def _():
        acc_ref[...] = jnp.zeros_like(acc_ref)

    acc_ref[...] += (jnp.dot(c_ref[...], p_ref[0], preferred_element_type=F32)
                     + jnp.dot(s_ref[...], q_ref[0], preferred_element_type=F32))

    @pl.when(k == pl.num_programs(2) - 1)
    def _():
        o_ref[0] = ((acc_ref[...] + bf_ref[...]) * sgf_ref[0].astype(F32)).astype(o_ref.dtype)


def _dft_matrices(n):
    n1 = 1
    while n1 * n1 < n:
        n1 *= 2
    n2 = n // n1
    m = jnp.arange(n, dtype=jnp.int32)[None, :]
    hi = jnp.arange(n2, dtype=jnp.int32)[:, None]
    lo = jnp.arange(n1, dtype=jnp.int32)[:, None]
    ang_hi = ((hi * n1 * m) % n).astype(F32) * (2.0 * math.pi / n)
    ang_lo = ((lo * m) % n).astype(F32) * (2.0 * math.pi / n)
    ch, sh_ = jnp.cos(ang_hi)[:, None, :], jnp.sin(ang_hi)[:, None, :]
    cl, sl = jnp.cos(ang_lo)[None, :, :], jnp.sin(ang_lo)[None, :, :]
    scale = n ** -0.5
    cmat = ((ch * cl - sh_ * sl) * scale).reshape(n, n).astype(BF16)
    smat = ((sh_ * cl + ch * sl) * (-scale)).reshape(n, n).astype(BF16)
    return cmat, smat


def _fourier_positions(p, q, b_f, sgf):
    bsz, n, width = p.shape
    cmat, smat = _dft_matrices(n)
    tm = _row_tile(n, 1024)
    tk = _row_tile(n, 1024)
    return pl.pallas_call(
        _dft_body,
        grid=(bsz, n // tm, n // tk),
        in_specs=[pl.BlockSpec((tm, tk), lambda b, i, k: (i, k)),
                  pl.BlockSpec((tm, tk), lambda b, i, k: (i, k)),
                  pl.BlockSpec((1, tk, width), lambda b, i, k: (b, k, 0)),
                  pl.BlockSpec((1, tk, width), lambda b, i, k: (b, k, 0)),
                  pl.BlockSpec((1, width), lambda b, i, k: (0, 0)),
                  pl.BlockSpec((1, tm, width), lambda b, i, k: (b, i, 0))],
        out_specs=pl.BlockSpec((1, tm, width), lambda b, i, k: (b, i, 0)),
        out_shape=jax.ShapeDtypeStruct((bsz, n, width), BF16),
        scratch_shapes=[pltpu.VMEM((tm, width), F32)],
        compiler_params=_params("parallel", "parallel", "arbitrary"),
        name="fourier_positions",
    )(cmat, smat, p, q, b_f.reshape(1, width), sgf)


def _fft_real(re, im):
    root_half = math.sqrt(0.5)

    def rec(re, im, want_im):
        n = len(re)
        if n == 1:
            return list(re), list(im)
        er, ei = rec(re[0::2], im[0::2], True)
        xr, xi = rec(re[1::2], im[1::2], True)
        out_r, out_i = [None] * n, [None] * n
        for k in range(n // 2):
            if k == 0:
                tr, ti = xr[k], xi[k]
            elif 4 * k == n:
                tr, ti = xi[k], None
            elif 8 * k == n:
                tr = (xr[k] + xi[k]) * root_half
                ti = (xi[k] - xr[k]) * root_half if want_im else None
            elif 8 * k == 3 * n:
                tr = (xi[k] - xr[k]) * root_half
                ti = (xr[k] + xi[k]) * (-root_half) if want_im else None
            else:
                c, s = math.cos(2 * math.pi * k / n), math.sin(2 * math.pi * k / n)
                tr = xr[k] * c + xi[k] * s
                ti = xi[k] * c - xr[k] * s if want_im else None
            out_r[k] = er[k] + tr
            out_r[k + n // 2] = er[k] - tr
            if want_im:
                if 4 * k == n:
                    out_i[k] = ei[k] - xr[k]
                    out_i[k + n // 2] = ei[k] + xr[k]
                else:
                    out_i[k] = ei[k] + ti
                    out_i[k + n // 2] = ei[k] - ti
        return out_r, out_i

    return rec(list(re), list(im), False)[0]


def _fft_body(p_ref, q_ref, wp_ref, wq_ref, bf_ref, sgf_ref, o_ref, yre_ref, yim_ref):
    radix, half, tc = yre_ref.shape
    for n1 in range(radix):
        y = (jnp.dot(wp_ref[n1], p_ref[0, n1], preferred_element_type=F32)
             + jnp.dot(wq_ref[n1], q_ref[0, n1], preferred_element_type=F32))
        yre_ref[n1] = y[:half]
        yim_ref[n1] = y[half:]

    rows = BF16_ROWS

    def chunk(it, carry):
        r0 = pl.multiple_of(it * rows, rows)
        for lt in range(tc // LANES):
            ls = slice(lt * LANES, (lt + 1) * LANES)
            re = [yre_ref[n1, pl.ds(r0, rows), ls] for n1 in range(radix)]
            im = [yim_ref[n1, pl.ds(r0, rows), ls] for n1 in range(radix)]
            out = _fft_real(re, im)
            for k1 in range(radix):
                gate = sgf_ref[0, k1, pl.ds(r0, rows), ls].astype(F32)
                o_ref[0, k1, pl.ds(r0, rows), ls] = ((out[k1] + bf_ref[:, ls]) * gate).astype(o_ref.dtype)
        return carry

    lax.fori_loop(0, half // rows, chunk, 0)


def _fft_matrices(n):
    half = n // FFT_RADIX
    k2 = jnp.arange(half, dtype=jnp.int32)[:, None]
    n1 = jnp.arange(FFT_RADIX, dtype=jnp.int32)[None, :]
    n2 = jnp.arange(half, dtype=jnp.int32)[None, :]
    ang1 = ((k2 * n1) % n).astype(F32) * (2.0 * math.pi / n)
    ang2 = ((k2 * n2) % half).astype(F32) * (2.0 * math.pi / half)
    c1, s1 = jnp.cos(ang1).T[:, :, None], jnp.sin(ang1).T[:, :, None]
    c2, s2 = jnp.cos(ang2)[None], jnp.sin(ang2)[None]
    scale = n ** -0.5
    cm = (c1 * c2 - s1 * s2) * scale
    sm = (s1 * c2 + c1 * s2) * scale
    wp = jnp.concatenate([cm, -sm], axis=1).astype(BF16)
    wq = jnp.concatenate([-sm, -cm], axis=1).astype(BF16)
    return wp, wq


def _fourier_positions_fft(pd, qd, b_f, sgf):
    bsz, radix, half, width = pd.shape
    n = radix * half
    wp, wq = _fft_matrices(n)
    tc = 2 * LANES
    data = pl.BlockSpec((1, radix, half, tc), lambda b, c: (b, 0, 0, c))
    mats = pl.BlockSpec((radix, 2 * half, half), lambda b, c: (0, 0, 0))
    out = pl.pallas_call(
        _fft_body,
        grid=(bsz, width // tc),
        in_specs=[data, data, mats, mats,
                  pl.BlockSpec((1, tc), lambda b, c: (0, c)),
                  data],
        out_specs=data,
        out_shape=jax.ShapeDtypeStruct((bsz, radix, half, width), BF16),
        scratch_shapes=[pltpu.VMEM((radix, half, tc), F32), pltpu.VMEM((radix, half, tc), F32)],
        compiler_params=_params("parallel", "arbitrary"),
        name="fourier_positions_fft",
    )(pd, qd, wp, wq, b_f.reshape(1, width), sgf.reshape(bsz, radix, half, width))
    return out.reshape(bsz, n, width)


def _l0_out_body(yf_ref, yb_ref, sg_ref, fm_ref, w_ref, g_ref, gt_ref, x_ref, o_ref):
    width = yf_ref.shape[-1]
    r = (yf_ref[0].astype(F32) + yb_ref[0].astype(F32)) * sg_ref[0].astype(F32)
    y = (jnp.dot(r.astype(BF16), w_ref[:width, :], preferred_element_type=F32)
         + jnp.dot(fm_ref[0], w_ref[width:, :], preferred_element_type=F32))
    o_ref[0] = x_ref[0] + gt_ref[0] * _rms(y, g_ref[...])


def _l0_out_proj(yf, yb, sg, fmix, w_out, g_post, gt, x, per_batch_mod):
    bsz, n, width = yf.shape
    d = x.shape[-1]
    t = _row_tile(n, 512)
    mod_idx = (lambda b, j: (b, 0, 0)) if per_batch_mod else (lambda b, j: (0, 0, 0))
    row = lambda b, j: (b, j, 0)
    return pl.pallas_call(
        _l0_out_body,
        grid=(bsz, n // t),
        in_specs=[pl.BlockSpec((1, t, width), row)] * 4
                 + [pl.BlockSpec(w_out.shape, lambda b, j: (0, 0)),
                    pl.BlockSpec((1, d), lambda b, j: (0, 0)),
                    pl.BlockSpec((1, 1, d), mod_idx),
                    pl.BlockSpec((1, t, d), row)],
        out_specs=pl.BlockSpec((1, t, d), row),
        out_shape=jax.ShapeDtypeStruct((bsz, n, d), F32),
        compiler_params=_params("parallel", "arbitrary"),
        name="l0_out_proj",
    )(yf, yb, sg, fmix, w_out, g_post.reshape(1, d), gt, x)


def _rope(z, cos, sin_signed):
    quarter = HEAD_DIM // 4
    lane = lax.broadcasted_iota(jnp.int32, (z.shape[0], LANES), 1)
    first = (lane % (2 * quarter)) < quarter
    chunks = []
    for c in range(z.shape[1] // LANES):
        zc = z[:, c * LANES:(c + 1) * LANES]
        swapped = jnp.where(first, pltpu.roll(zc, LANES - quarter, 1), pltpu.roll(zc, quarter, 1))
        chunks.append(zc * cos + swapped * sin_signed)
    return jnp.concatenate(chunks, axis=1)


def _l1_in_body(x_ref, sc_ref, sh_ref, g_ref, w_ref, cos_ref, sin_ref, q_ref, k_ref, v_ref, sg_ref):
    qw, kw = q_ref.shape[-1], k_ref.shape[-1]
    h = _rms(x_ref[0], g_ref[...]) * (1.0 + sc_ref[0]) + sh_ref[0]
    hb = h.astype(BF16)
    cos, sin = cos_ref[...], sin_ref[...]
    zq = jnp.dot(hb, w_ref[:, :qw], preferred_element_type=F32)
    q_ref[0] = (_rope(zq, cos, sin) * (HEAD_DIM ** -0.5 * LOG2E)).astype(q_ref.dtype)
    zk = jnp.dot(hb, w_ref[:, qw:qw + kw], preferred_element_type=F32)
    k_ref[0] = _rope(zk, cos, sin).astype(k_ref.dtype)
    zv = jnp.dot(hb, w_ref[:, qw + kw:qw + 2 * kw], preferred_element_type=F32)
    v_ref[0] = zv.astype(v_ref.dtype)
    zg = jnp.dot(hb, w_ref[:, qw + 2 * kw:], preferred_element_type=F32)
    sg_ref[0] = _silu(zg).astype(sg_ref.dtype)


def _rope_tables(n):
    t = jnp.arange(n)
    row = (t // GRID_W).astype(F32)
    col = (t % GRID_W).astype(F32)
    half = HEAD_DIM // 2
    freqs = ROPE_BASE ** (-jnp.arange(0, half, 2, dtype=F32) / half)
    ar = row[:, None] * freqs[None, :]
    ac = col[:, None] * freqs[None, :]
    cos = jnp.concatenate([jnp.cos(ar), jnp.cos(ar), jnp.cos(ac), jnp.cos(ac)], axis=1)
    sin = jnp.concatenate([-jnp.sin(ar), jnp.sin(ar), -jnp.sin(ac), jnp.sin(ac)], axis=1)
    reps = LANES // HEAD_DIM
    return jnp.tile(cos, (1, reps)), jnp.tile(sin, (1, reps))


def _l1_in_proj(x, sc, sh, g_pre, w_in, q_width, kv_width):
    bsz, n, d = x.shape
    t = _row_tile(n, 512)
    cos, sin = _rope_tables(n)
    row = lambda b, j: (b, j, 0)
    mod_idx = lambda b, j: (b, 0, 0)
    return pl.pallas_call(
        _l1_in_body,
        grid=(bsz, n // t),
        in_specs=[pl.BlockSpec((1, t, d), row),
                  pl.BlockSpec((1, 1, d), mod_idx),
                  pl.BlockSpec((1, 1, d), mod_idx),
                  pl.BlockSpec((1, d), lambda b, j: (0, 0)),
                  pl.BlockSpec(w_in.shape, lambda b, j: (0, 0)),
                  pl.BlockSpec((t, LANES), lambda b, j: (j, 0)),
                  pl.BlockSpec((t, LANES), lambda b, j: (j, 0))],
        out_specs=[pl.BlockSpec((1, t, q_width), row),
                   pl.BlockSpec((1, t, kv_width), row),
                   pl.BlockSpec((1, t, kv_width), row),
                   pl.BlockSpec((1, t, q_width), row)],
        out_shape=[jax.ShapeDtypeStruct((bsz, n, q_width), BF16),
                   jax.ShapeDtypeStruct((bsz, n, kv_width), BF16),
                   jax.ShapeDtypeStruct((bsz, n, kv_width), BF16),
                   jax.ShapeDtypeStruct((bsz, n, q_width), BF16)],
        compiler_params=_params("parallel", "arbitrary"),
        name="l1_in_proj",
    )(x, sc, sh, g_pre.reshape(1, d), w_in, cos, sin)


def _ctx_kv_body(x_ref, sc_ref, sh_ref, g_ref, w_ref, k_ref, v_ref):
    kw = k_ref.shape[-1]
    h = _rms(x_ref[0], g_ref[...]) * (1.0 + sc_ref[0]) + sh_ref[0]
    z = jnp.dot(h.astype(BF16), w_ref[...], preferred_element_type=F32)
    k_ref[0] = z[:, :kw].astype(k_ref.dtype)
    v_ref[0] = z[:, kw:].astype(v_ref.dtype)


def _ctx_kv_proj(ctx, sc, sh, g_pre, w_kv):
    bsz, n, d = ctx.shape
    kw = w_kv.shape[1] // 2
    row = lambda b: (b, 0, 0)
    shared = lambda b: (0, 0, 0)
    out = jax.ShapeDtypeStruct((bsz, n, kw), BF16)
    return pl.pallas_call(
        _ctx_kv_body,
        grid=(bsz,),
        in_specs=[pl.BlockSpec((1, n, d), row),
                  pl.BlockSpec((1, 1, d), shared),
                  pl.BlockSpec((1, 1, d), shared),
                  pl.BlockSpec((1, d), lambda b: (0, 0)),
                  pl.BlockSpec(w_kv.shape, lambda b: (0, 0))],
        out_specs=[pl.BlockSpec((1, n, kw), row)] * 2,
        out_shape=[out, out],
        compiler_params=_params("arbitrary"),
        name="ctx_kv_proj",
    )(ctx, sc, sh, g_pre.reshape(1, d), w_kv)


def _attn_body(sink_ref, q_ref, km_ref, kp_ref, kn_ref, vm_ref, vp_ref, vn_ref, kc_ref, vc_ref,
               sg_ref, w_ref, g_ref, gt_ref, x_ref, o_ref, kf_ref, vf_ref, att_ref):
    i = pl.program_id(1)
    ni = pl.num_programs(1)
    tq = q_ref.shape[1]
    blk = ATTN_BLOCK
    per_step = tq // blk
    group = q_ref.shape[-1] // HEAD_DIM // N_KV_HEADS

    kf_ref[0:blk, :] = kp_ref[0]
    kf_ref[blk:blk + tq, :] = km_ref[0]
    kf_ref[blk + tq:, :] = kn_ref[0]
    vf_ref[0:blk, :] = vp_ref[0]
    vf_ref[blk:blk + tq, :] = vm_ref[0]
    vf_ref[blk + tq:, :] = vn_ref[0]

    ri = lax.broadcasted_iota(jnp.int32, (blk, blk), 0)
    cj = lax.broadcasted_iota(jnp.int32, (blk, blk), 1)
    tri_prev = jnp.where(cj >= ri, 0.0, NEG_INF)
    tri_next = jnp.where(cj <= ri, 0.0, NEG_INF)
    nt_dims = (((1,), (1,)), ((), ()))

    for qb in range(per_step):
        r0 = qb * blk
        prev_bias = tri_prev + jnp.where((i * per_step + qb) > 0, 0.0, NEG_INF)
        next_bias = tri_next + jnp.where((i * per_step + qb) < ni * per_step - 1, 0.0, NEG_INF)
        for kh in range(N_KV_HEADS):
            hs = slice(kh * HEAD_DIM, (kh + 1) * HEAD_DIM)
            q4 = jnp.concatenate(
                [q_ref[0, r0:r0 + blk, (kh * group + g) * HEAD_DIM:(kh * group + g + 1) * HEAD_DIM]
                 for g in range(group)], axis=0)
            s_lat = lax.dot_general(q4, kf_ref[r0:r0 + 3 * blk, hs], nt_dims,
                                    preferred_element_type=F32)
            s_ctx = lax.dot_general(q4, kc_ref[0, :, hs], nt_dims, preferred_element_type=F32)
            v_lat = vf_ref[r0:r0 + 3 * blk, hs]
            v_ctx = vc_ref[0, :, hs]
            outs = []
            for g in range(group):
                rows = slice(g * blk, (g + 1) * blk)
                sink = sink_ref[kh * group + g]
                cols = [s_lat[rows, :blk] + prev_bias, s_lat[rows, blk:2 * blk],
                        s_lat[rows, 2 * blk:] + next_bias]
                cols += [s_ctx[rows, c * blk:(c + 1) * blk] for c in range(s_ctx.shape[1] // blk)]
                mx = functools.reduce(jnp.maximum, cols)
                m = jnp.maximum(jnp.max(mx, axis=-1, keepdims=True), sink)
                e = [jnp.exp2(c - m) for c in cols]
                denom = (jnp.sum(functools.reduce(jnp.add, e), axis=-1, keepdims=True)
                         + jnp.exp2(sink - m))
                og = (jnp.dot(jnp.concatenate(e[:3], axis=1).astype(BF16), v_lat,
                              preferred_element_type=F32)
                      + jnp.dot(jnp.concatenate(e[3:], axis=1).astype(BF16), v_ctx,
                                preferred_element_type=F32))
                outs.append(og / denom)
            att_ref[r0:r0 + blk, kh * group * HEAD_DIM:(kh + 1) * group * HEAD_DIM] = (
                jnp.concatenate(outs, axis=1))

    gated = (att_ref[...] * sg_ref[0].astype(F32)).astype(BF16)
    y = jnp.dot(gated, w_ref[...], preferred_element_type=F32)
    o_ref[0] = x_ref[0] + gt_ref[0] * _rms(y, g_ref[...])


def _attention_out(q, k, v, kc, vc, sg, sink, w_out, g_post, gt, x):
    bsz, n, qw = q.shape
    kw = k.shape[-1]
    nc = kc.shape[1]
    d = x.shape[-1]
    blk = ATTN_BLOCK
    tq = _row_tile(n, 512)
    per = tq // blk
    nb = n // blk
    row = lambda b, i: (b, i, 0)
    prev = lambda b, i: (b, jnp.maximum(i * per - 1, 0), 0)
    nxt = lambda b, i: (b, jnp.minimum((i + 1) * per, nb - 1), 0)
    batch = lambda b, i: (b, 0, 0)
    kv_specs = [pl.BlockSpec((1, tq, kw), row), pl.BlockSpec((1, blk, kw), prev),
                pl.BlockSpec((1, blk, kw), nxt)]
    return pl.pallas_call(
        _attn_body,
        grid=(bsz, n // tq),
        in_specs=[pl.BlockSpec(memory_space=pltpu.SMEM),
                  pl.BlockSpec((1, tq, qw), row)]
                 + kv_specs + kv_specs
                 + [pl.BlockSpec((1, nc, kw), batch), pl.BlockSpec((1, nc, kw), batch),
                    pl.BlockSpec((1, tq, qw), row),
                    pl.BlockSpec(w_out.shape, lambda b, i: (0, 0)),
                    pl.BlockSpec((1, d), lambda b, i: (0, 0)),
                    pl.BlockSpec((1, 1, d), batch),
                    pl.BlockSpec((1, tq, d), row)],
        out_specs=pl.BlockSpec((1, tq, d), row),
        out_shape=jax.ShapeDtypeStruct((bsz, n, d), F32),
        scratch_shapes=[pltpu.VMEM((tq + 2 * blk, kw), BF16),
                        pltpu.VMEM((tq + 2 * blk, kw), BF16),
                        pltpu.VMEM((tq, qw), F32)],
        compiler_params=_params("parallel", "arbitrary"),
        name="window_attention_out",
    )(sink, q, k, k, k, v, v, v, kc, vc, sg, w_out, g_post.reshape(1, d), gt, x)


def _mod_vectors(c, c_ctx, w_mod, b_mod):
    bsz, d = c.shape
    rows = -(-(bsz + 1) // SUBLANES) * SUBLANES
    cvec = jnp.zeros((rows, d), F32).at[:bsz].set(c).at[bsz].set(c_ctx)
    m = _modulation(cvec, w_mod, b_mod)
    lat = [m[:bsz, k * d:(k + 1) * d].reshape(bsz, 1, d) for k in range(3)]
    cx = [m[bsz:bsz + 1, k * d:(k + 1) * d].reshape(1, 1, d) for k in range(3)]
    return lat, cx


def kernel(x, c, ctx, c_ctx, l0_w_mod, l0_b_mod, l0_g_pre, l0_g_post, l0_w_in, l0_w_conv, l0_b_conv,
           l0_w_a, l0_b_a, l0_w_x, l0_b_x, l0_lam, l0_w_f, l0_b_f, l0_w_out,
           l1_w_mod, l1_b_mod, l1_g_pre, l1_g_post, l1_w_in, l1_sink, l1_w_out):
    bsz, n, d = x.shape
    lru_w = l0_w_conv.shape[1]
    f_w = l0_w_f.shape[0] * l0_w_f.shape[1]
    assert lru_w == f_w == d

    (sh_l, sc_l, gt_l), (sh_c, sc_c, gt_c) = _mod_vectors(c, c_ctx, l0_w_mod, l0_b_mod)
    wp, wq = _fold_fourier_weights(l0_w_in, l0_w_f, 2 * lru_w)
    w_cat = jnp.concatenate([l0_w_in[:, :2 * lru_w], wp, wq, l0_w_in[:, 2 * lru_w + f_w:]],
                            axis=1).astype(BF16)
    wg = _gate_weights(l0_w_a, l0_w_x)
    w_out0 = l0_w_out.astype(BF16)

    use_fft = n % (FFT_RADIX * 32) == 0 and n >= 1024
    conv = (l0_w_conv, l0_b_conv)
    u_c, sg_c, p_c, q_c, sgf_c = _l0_in_proj(ctx, sc_c, sh_c, l0_g_pre, w_cat, *conv, False, False)
    u_l, sg_l, p_l, q_l, sgf_l = _l0_in_proj(x, sc_l, sh_l, l0_g_pre, w_cat, *conv, True, use_fft)

    zeros = jnp.zeros((bsz, 1, lru_w), F32)
    lru_args = (wg, l0_b_a, l0_b_x, l0_lam)
    yf_c, yb_c, hf_c, hb_c = _rglru(u_c, *lru_args, zeros, zeros)
    yf_l, yb_l, _, _ = _rglru(u_l, *lru_args, hf_c, hb_c)

    fm_c = _fourier_positions(p_c, q_c, l0_b_f, sgf_c)
    fm_l = (_fourier_positions_fft if use_fft else _fourier_positions)(p_l, q_l, l0_b_f, sgf_l)

    ctx1 = _l0_out_proj(yf_c, yb_c, sg_c, fm_c, w_out0, l0_g_post, gt_c, ctx, False)
    x1 = _l0_out_proj(yf_l, yb_l, sg_l, fm_l, w_out0, l0_g_post, gt_l, x, True)

    (sh_l, sc_l, gt_l), (sh_c, sc_c, _) = _mod_vectors(c, c_ctx, l1_w_mod, l1_b_mod)
    q_width = l1_w_out.shape[0]
    kv_width = (l1_w_in.shape[1] - 2 * q_width) // 2
    w_in1 = l1_w_in.astype(BF16)
    kc, vc = _ctx_kv_proj(ctx1, sc_c, sh_c, l1_g_pre, w_in1[:, q_width:q_width + 2 * kv_width])
    q, k, v, sg1 = _l1_in_proj(x1, sc_l, sh_l, l1_g_pre, w_in1, q_width, kv_width)
    return _attention_out(q, k, v, kc, vc, sg1, l1_sink * LOG2E, l1_w_out.astype(BF16), l1_g_post,
                          gt_l, x1)
```

```python
import functools
import math

import jax
import jax.numpy as jnp
import numpy as np
from jax import lax
from jax.experimental import pallas as pl
from jax.experimental.pallas import tpu as pltpu

EPS = 1e-6
NEG_INF = -1e30
LRU_HEADS = 16
LRU_C = 8.0
CONV_W = 4
CONV_LEFT = 2
FNET_GROUPS = 8
HEAD_DIM = 64
N_KV_HEADS = 4
WINDOW = 128
ATTN_BLOCK = 128
GRID_W = 64
ROPE_BASE = 10000.0
LOG2E = math.log2(math.e)

LANES = 128
SUBLANES = 8
BF16_ROWS = 16
FFT_RADIX = 16
GATE_COLS = 256
SCAN_SEG = 4
VMEM_LIMIT = 56 * 1024 * 1024

BF16 = jnp.bfloat16
F32 = jnp.float32


def _params(*sem):
    return pltpu.CompilerParams(dimension_semantics=sem, vmem_limit_bytes=VMEM_LIMIT)


def _sigmoid(z):
    return 0.5 * jnp.tanh(0.5 * z) + 0.5


def _silu(z):
    return z * _sigmoid(z)


def _silu_of_double(zh):
    return zh * jnp.tanh(zh) + zh


def _rms(v, g):
    return v * lax.rsqrt(jnp.mean(v * v, axis=-1, keepdims=True) + EPS) * g


def _row_tile(n, want):
    t = min(n, want)
    assert n % t == 0
    return t


def _mod_body(c_ref, w_ref, b_ref, o_ref):
    o_ref[...] = jnp.dot(_silu(c_ref[...]), w_ref[...], preferred_element_type=F32,
                         precision=lax.Precision.HIGHEST) + b_ref[...]


def _modulation(cvec, w_mod, b_mod):
    rows, d = cvec.shape
    width = w_mod.shape[1]
    tn = _row_tile(width, 1024)
    return pl.pallas_call(
        _mod_body,
        grid=(width // tn,),
        in_specs=[pl.BlockSpec((rows, d), lambda j: (0, 0)),
                  pl.BlockSpec((d, tn), lambda j: (0, j)),
                  pl.BlockSpec((1, tn), lambda j: (0, j))],
        out_specs=pl.BlockSpec((rows, tn), lambda j: (0, j)),
        out_shape=jax.ShapeDtypeStruct((rows, width), F32),
        compiler_params=_params("arbitrary"),
        name="modulation",
    )(cvec, w_mod, b_mod.reshape(1, width))


def _fold_body(win_ref, wf_ref, cd_ref, sd_ref, wp_ref, wq_ref):
    hi = lax.Precision.HIGHEST
    win = win_ref[...]
    wf = wf_ref[0]
    cw = jnp.dot(cd_ref[...], wf, preferred_element_type=F32, precision=hi)
    sw = jnp.dot(sd_ref[...], wf, preferred_element_type=F32, precision=hi)
    wp_ref[...] = jnp.dot(win, cw, preferred_element_type=F32, precision=hi)
    wq_ref[...] = jnp.dot(win, sw, preferred_element_type=F32, precision=hi)


def _fold_fourier_weights(w_in, w_f, col0):
    d = w_in.shape[0]
    groups, gd, _ = w_f.shape
    idx = np.arange(gd)
    ang = 2.0 * np.pi * ((idx[:, None] * idx[None, :]) % gd) / gd
    cd = jnp.asarray(np.cos(ang) / math.sqrt(gd), F32)
    sd = jnp.asarray(np.sin(ang) / math.sqrt(gd), F32)
    out = jax.ShapeDtypeStruct((d, groups * gd), F32)
    return pl.pallas_call(
        _fold_body,
        grid=(groups,),
        in_specs=[pl.BlockSpec((d, gd), lambda g: (0, col0 // gd + g)),
                  pl.BlockSpec((1, gd, gd), lambda g: (g, 0, 0)),
                  pl.BlockSpec((gd, gd), lambda g: (0, 0)),
                  pl.BlockSpec((gd, gd), lambda g: (0, 0))],
        out_specs=[pl.BlockSpec((d, gd), lambda g: (0, g)),
                   pl.BlockSpec((d, gd), lambda g: (0, g))],
        out_shape=[out, out],
        compiler_params=_params("arbitrary"),
        name="fold_fourier_weights",
    )(w_in, w_f, cd, sd)


def _l0_in_body(decimate, x_ref, xp_ref, xn_ref, sc_ref, sh_ref, g_ref, w_ref, wconv_ref, bconv_ref,
                u_ref, sg_ref, p_ref, q_ref, sgf_ref, *slab):
    j = pl.program_id(1)
    width = u_ref.shape[-1]
    t = x_ref.shape[1]

    def modulated(v):
        return _rms(v, g_ref[...]) * (1.0 + sc_ref[0]) + sh_ref[0]

    h = modulated(x_ref[0])
    hb = h.astype(BF16)

    h_prev = modulated(xp_ref[0]) * jnp.where(j > 0, 1.0, 0.0)
    h_next = modulated(xn_ref[0]) * jnp.where(j < pl.num_programs(1) - 1, 1.0, 0.0)
    h_ext = jnp.concatenate([h_prev, h, h_next], axis=0).astype(BF16)
    z_ext = jnp.dot(h_ext, w_ref[:, :width], preferred_element_type=F32)
    uc = bconv_ref[...] + jnp.zeros((t, width), F32)
    for k in range(CONV_W):
        shift = (CONV_LEFT - k) % (t + 2 * SUBLANES)
        tap = z_ext if shift == 0 else pltpu.roll(z_ext, shift, 0)
        uc = uc + tap[SUBLANES:SUBLANES + t] * wconv_ref[k:k + 1, :]
    u_ref[0] = uc.astype(u_ref.dtype)

    hb_dec = hb
    if decimate:
        slab_ref, = slab
        per = p_ref.shape[2]
        d = h.shape[-1]
        for c in range(d // LANES):
            slab_ref[c] = h[:, c * LANES:(c + 1) * LANES]
        hb_dec = jnp.concatenate(
            [jnp.concatenate([slab_ref[c, pl.ds(n1, per, stride=FFT_RADIX), :]
                              for c in range(d // LANES)], axis=1)
             for n1 in range(FFT_RADIX)], axis=0).astype(BF16)
    outs = (None, sg_ref, p_ref, q_ref, sgf_ref)
    gated = (False, True, False, False, True)
    for part, (o_ref, is_gate) in enumerate(zip(outs, gated)):
        if o_ref is None:
            continue
        lhs = hb_dec if part in (2, 3) else hb
        z = jnp.dot(lhs, w_ref[:, part * width:(part + 1) * width], preferred_element_type=F32)
        if is_gate:
            z = _silu_of_double(z)
        o_ref[...] = z.astype(o_ref.dtype).reshape(o_ref.shape)


def _l0_in_proj(x, sc, sh, g_pre, w_cat, w_conv, b_conv, per_batch_mod, decimate):
    bsz, n, d = x.shape
    width = w_cat.shape[1] // 5
    t = _row_tile(n, 512)
    per8 = t // SUBLANES
    last8 = n // SUBLANES - 1
    mod_idx = (lambda b, j: (b, 0, 0)) if per_batch_mod else (lambda b, j: (0, 0, 0))
    out = jax.ShapeDtypeStruct((bsz, n, width), BF16)
    row_spec = pl.BlockSpec((1, t, width), lambda b, j: (b, j, 0))
    if decimate:
        dec = jax.ShapeDtypeStruct((bsz, FFT_RADIX, n // FFT_RADIX, width), BF16)
        dec_spec = pl.BlockSpec((1, FFT_RADIX, t // FFT_RADIX, width), lambda b, j: (b, 0, j, 0))
        out_specs = [row_spec, row_spec, dec_spec, dec_spec, row_spec]
        out_shape = [out, out, dec, dec, out]
        scratch = [pltpu.VMEM((d // LANES, t, LANES), F32)]
    else:
        out_specs, out_shape, scratch = [row_spec] * 5, [out] * 5, []
    return pl.pallas_call(
        functools.partial(_l0_in_body, decimate),
        grid=(bsz, n // t),
        in_specs=[pl.BlockSpec((1, t, d), lambda b, j: (b, j, 0)),
                  pl.BlockSpec((1, SUBLANES, d), lambda b, j: (b, jnp.maximum(j * per8 - 1, 0), 0)),
                  pl.BlockSpec((1, SUBLANES, d),
                               lambda b, j: (b, jnp.minimum((j + 1) * per8, last8), 0)),
                  pl.BlockSpec((1, 1, d), mod_idx),
                  pl.BlockSpec((1, 1, d), mod_idx),
                  pl.BlockSpec((1, d), lambda b, j: (0, 0)),
                  pl.BlockSpec(w_cat.shape, lambda b, j: (0, 0)),
                  pl.BlockSpec(w_conv.shape, lambda b, j: (0, 0)),
                  pl.BlockSpec((1, width), lambda b, j: (0, 0))],
        out_specs=out_specs,
        out_shape=out_shape,
        scratch_shapes=scratch,
        compiler_params=_params("parallel", "arbitrary"),
        name="l0_in_proj",
    )(x, x, x, sc, sh, g_pre.reshape(1, d), w_cat, w_conv, b_conv.reshape(1, width))


def _lru_direction(rev, scan_start, u_ref, wg_ref, ba_ref, bx_ref, lam_ref, h0_ref, y_ref, hfin_ref,
                   a_ref, b_ref, carry_ref):
    _, t, width = u_ref.shape
    d = 1 if rev else 0

    z = -lam_ref[d:d + 1, :]
    c_half = (-0.5 * LRU_C) * (jnp.maximum(z, 0.0) + jnp.log1p(jnp.exp(-jnp.abs(z))))
    ba_half = 0.5 * ba_ref[d:d + 1, :]
    bx_half = 0.5 * bx_ref[d:d + 1, :]

    for cg in range(width // GATE_COLS):
        cols = slice(cg * GATE_COLS, (cg + 1) * GATE_COLS)
        ub = u_ref[0, :, cols]
        zg = jnp.dot(ub, wg_ref[d, cg], preferred_element_type=F32)
        ta = jnp.tanh(zg[:, :GATE_COLS] + ba_half[:, cols])
        ti = jnp.tanh(zg[:, GATE_COLS:] + bx_half[:, cols])
        log_a = c_half[:, cols] * ta + c_half[:, cols]
        a = jnp.exp(log_a)
        v = -jnp.tanh(log_a) * (1.0 + a * a)
        amp = jnp.where(v > 0.0, v * lax.rsqrt(v), 0.0)
        u_half = 0.5 * ub.astype(F32)
        bt = amp * (u_half * ti + u_half)
        for piece in range(GATE_COLS // LANES):
            lanes = slice(piece * LANES, (piece + 1) * LANES)
            a_ref[cg * (GATE_COLS // LANES) + piece] = a[:, lanes]
            b_ref[cg * (GATE_COLS // LANES) + piece] = bt[:, lanes]

    @pl.when(scan_start)
    def _():
        carry_ref[d:d + 1, :] = h0_ref[0]

    slabs = width // LANES
    blk = SCAN_SEG * SUBLANES
    nblk = t // blk
    row = lax.broadcasted_iota(jnp.int32, (SUBLANES, LANES), 0)
    order = tuple(range(SCAN_SEG - 1, -1, -1) if rev else range(SCAN_SEG))

    def body(it, hs):
        g = (nblk - 1 - it) if rev else it
        base = pl.multiple_of(g * blk, blk)
        out = []
        for c in range(slabs):
            h_in = hs[c]
            av = [a_ref[c, pl.ds(base + i, SUBLANES, stride=SCAN_SEG), :] for i in range(SCAN_SEG)]
            bv = [b_ref[c, pl.ds(base + i, SUBLANES, stride=SCAN_SEG), :] for i in range(SCAN_SEG)]
            prod, acc = {order[0]: av[order[0]]}, {order[0]: bv[order[0]]}
            for before, i in zip(order[:-1], order[1:]):
                acc[i] = av[i] * acc[before] + bv[i]
                prod[i] = av[i] * prod[before]
            pc, hc = prod[order[-1]], acc[order[-1]]
            for s in (1, 2, 4):
                if rev:
                    shift, m = SUBLANES - s, row < SUBLANES - s
                else:
                    shift, m = s, row >= s
                p_sh = pltpu.roll(pc, shift, 0)
                h_sh = pltpu.roll(hc, shift, 0)
                hc = jnp.where(m, pc * h_sh + hc, hc)
                pc = jnp.where(m, pc * p_sh, pc)
            h_end = hc + pc * h_in
            if rev:
                h_start = jnp.where(row < SUBLANES - 1, pltpu.roll(h_end, SUBLANES - 1, 0), h_in)
            else:
                h_start = jnp.where(row >= 1, pltpu.roll(h_end, 1, 0), h_in)
            for i in range(SCAN_SEG):
                b_ref[c, pl.ds(base + i, SUBLANES, stride=SCAN_SEG), :] = acc[i] + prod[i] * h_start
            out.append(h_end[0:1, :] if rev else h_end[SUBLANES - 1:SUBLANES, :])
        return tuple(out)

    init = tuple(carry_ref[d:d + 1, c * LANES:(c + 1) * LANES] for c in range(slabs))
    h_last = jnp.concatenate(lax.fori_loop(0, nblk, body, init, unroll=2), axis=1)
    carry_ref[d:d + 1, :] = h_last
    hfin_ref[0] = h_last
    for c in range(slabs):
        y_ref[0, :, c * LANES:(c + 1) * LANES] = b_ref[c].astype(y_ref.dtype)


def _lru_body(uf_ref, ub_ref, wg_ref, ba_ref, bx_ref, lam_ref, h0f_ref, h0b_ref,
              yf_ref, yb_ref, hf_ref, hb_ref, a_ref, b_ref, carry_ref):
    first = pl.program_id(1) == 0
    common = (wg_ref, ba_ref, bx_ref, lam_ref)
    scratch = (a_ref, b_ref, carry_ref)
    _lru_direction(False, first, uf_ref, *common, h0f_ref, yf_ref, hf_ref, *scratch)
    _lru_direction(True, first, ub_ref, *common, h0b_ref, yb_ref, hb_ref, *scratch)


def _rglru(u, wg, b_a, b_x, lam, h0f, h0b):
    bsz, n, width = u.shape
    t = _row_tile(n, 512)
    nt = n // t
    fwd = lambda b, j: (b, j, 0)
    bwd = lambda b, j: (b, nt - 1 - j, 0)
    tile = (1, t, width)
    const2 = lambda b, j: (0, 0)
    state = pl.BlockSpec((1, 1, width), lambda b, j: (b, 0, 0))
    y_shape = jax.ShapeDtypeStruct((bsz, n, width), BF16)
    h_shape = jax.ShapeDtypeStruct((bsz, 1, width), F32)
    return pl.pallas_call(
        _lru_body,
        grid=(bsz, nt),
        in_specs=[pl.BlockSpec(tile, fwd), pl.BlockSpec(tile, bwd),
                  pl.BlockSpec(wg.shape, lambda b, j: (0, 0, 0, 0)),
                  pl.BlockSpec(b_a.shape, const2),
                  pl.BlockSpec(b_x.shape, const2),
                  pl.BlockSpec(lam.shape, const2),
                  state, state],
        out_specs=[pl.BlockSpec(tile, fwd), pl.BlockSpec(tile, bwd), state, state],
        out_shape=[y_shape, y_shape, h_shape, h_shape],
        scratch_shapes=[pltpu.VMEM((width // LANES, t, LANES), F32),
                        pltpu.VMEM((width // LANES, t, LANES), F32),
                        pltpu.VMEM((2, width), F32)],
        compiler_params=_params("parallel", "arbitrary"),
        name="rglru_scan",
    )(u, u, wg, b_a, b_x, lam, h0f, h0b)


def _gate_weights(w_a, w_x):
    _, heads, blk, _ = w_a.shape
    per = GATE_COLS // blk
    groups = heads // per
    eye = jnp.eye(per, dtype=w_a.dtype)

    def bd(w):
        w = w.reshape(2, groups, per, blk, blk)
        full = jnp.einsum('dgpij,pq->dgpiqj', w, eye)
        return full.reshape(2, groups, GATE_COLS, GATE_COLS)

    return (0.5 * jnp.concatenate([bd(w_a), bd(w_x)], axis=-1)).astype(BF16)


def _dft_body(c_ref, s_ref, p_ref, q_ref, bf_ref, sgf_ref, o_ref, acc_ref):
    k = pl.program_id(2)

    @pl.when(k == 0)
    def _():
        acc_ref[...] = jnp.zeros_like(acc_ref)

    acc_ref[...] += (jnp.dot(c_ref[...], p_ref[0], preferred_element_type=F32)
                     + jnp.dot(s_ref[...], q_ref[0], preferred_element_type=F32))

    @pl.when(k == pl.num_programs(2) - 1)
    def _():
        o_ref[0] = ((acc_ref[...] + bf_ref[...]) * sgf_ref[0].astype(F32)).astype(o_ref.dtype)


def _dft_matrices(n):
    n1 = 1
    while n1 * n1 < n:
        n1 *= 2
    n2 = n // n1
    m = jnp.arange(n, dtype=jnp.int32)[None, :]
    hi = jnp.arange(n2, dtype=jnp.int32)[:, None]
    lo = jnp.arange(n1, dtype=jnp.int32)[:, None]
    ang_hi = ((hi * n1 * m) % n).astype(F32) * (2.0 * math.pi / n)
    ang_lo = ((lo * m) % n).astype(F32) * (2.0 * math.pi / n)
    ch, sh_ = jnp.cos(ang_hi)[:, None, :], jnp.sin(ang_hi)[:, None, :]
    cl, sl = jnp.cos(ang_lo)[None, :, :], jnp.sin(ang_lo)[None, :, :]
    scale = n ** -0.5
    cmat = ((ch * cl - sh_ * sl) * scale).reshape(n, n).astype(BF16)
    smat = ((sh_ * cl + ch * sl) * (-scale)).reshape(n, n).astype(BF16)
    return cmat, smat


def _fourier_positions(p, q, b_f, sgf):
    bsz, n, width = p.shape
    cmat, smat = _dft_matrices(n)
    tm = _row_tile(n, 1024)
    tk = _row_tile(n, 1024)
    return pl.pallas_call(
        _dft_body,
        grid=(bsz, n // tm, n // tk),
        in_specs=[pl.BlockSpec((tm, tk), lambda b, i, k: (i, k)),
                  pl.BlockSpec((tm, tk), lambda b, i, k: (i, k)),
                  pl.BlockSpec((1, tk, width), lambda b, i, k: (b, k, 0)),
                  pl.BlockSpec((1, tk, width), lambda b, i, k: (b, k, 0)),
                  pl.BlockSpec((1, width), lambda b, i, k: (0, 0)),
                  pl.BlockSpec((1, tm, width), lambda b, i, k: (b, i, 0))],
        out_specs=pl.BlockSpec((1, tm, width), lambda b, i, k: (b, i, 0)),
        out_shape=jax.ShapeDtypeStruct((bsz, n, width), BF16),
        scratch_shapes=[pltpu.VMEM((tm, width), F32)],
        compiler_params=_params("parallel", "parallel", "arbitrary"),
        name="fourier_positions",
    )(cmat, smat, p, q, b_f.reshape(1, width), sgf)


def _fft_real(re, im):
    root_half = math.sqrt(0.5)

    def rec(re, im, want_im):
        n = len(re)
        if n == 1:
            return list(re), list(im)
        er, ei = rec(re[0::2], im[0::2], True)
        xr, xi = rec(re[1::2], im[1::2], True)
        out_r, out_i = [None] * n, [None] * n
        for k in range(n // 2):
            if k == 0:
                tr, ti = xr[k], xi[k]
            elif 4 * k == n:
                tr, ti = xi[k], None
            elif 8 * k == n:
                tr = (xr[k] + xi[k]) * root_half
                ti = (xi[k] - xr[k]) * root_half if want_im else None
            elif 8 * k == 3 * n:
                tr = (xi[k] - xr[k]) * root_half
                ti = (xr[k] + xi[k]) * (-root_half) if want_im else None
            else:
                c, s = math.cos(2 * math.pi * k / n), math.sin(2 * math.pi * k / n)
                tr = xr[k] * c + xi[k] * s
                ti = xi[k] * c - xr[k] * s if want_im else None
            out_r[k] = er[k] + tr
            out_r[k + n // 2] = er[k] - tr
            if want_im:
                if 4 * k == n:
                    out_i[k] = ei[k] - xr[k]
                    out_i[k + n // 2] = ei[k] + xr[k]
                else:
                    out_i[k] = ei[k] + ti
                    out_i[k + n // 2] = ei[k] - ti
        return out_r, out_i

    return rec(list(re), list(im), False)[0]


def _fft_body(p_ref, q_ref, wp_ref, wq_ref, bf_ref, sgf_ref, o_ref, yre_ref, yim_ref):
    radix, half, tc = yre_ref.shape
    for n1 in range(radix):
        y = (jnp.dot(wp_ref[n1], p_ref[0, n1], preferred_element_type=F32)
             + jnp.dot(wq_ref[n1], q_ref[0, n1], preferred_element_type=F32))
        yre_ref[n1] = y[:half]
        yim_ref[n1] = y[half:]

    rows = BF16_ROWS

    def chunk(it, carry):
        r0 = pl.multiple_of(it * rows, rows)
        for lt in range(tc // LANES):
            ls = slice(lt * LANES, (lt + 1) * LANES)
            re = [yre_ref[n1, pl.ds(r0, rows), ls] for n1 in range(radix)]
            im = [yim_ref[n1, pl.ds(r0, rows), ls] for n1 in range(radix)]
            out = _fft_real(re, im)
            for k1 in range(radix):
                gate = sgf_ref[0, k1, pl.ds(r0, rows), ls].astype(F32)
                o_ref[0, k1, pl.ds(r0, rows), ls] = ((out[k1] + bf_ref[:, ls]) * gate).astype(o_ref.dtype)
        return carry

    lax.fori_loop(0, half // rows, chunk, 0)


def _fft_matrices(n):
    half = n // FFT_RADIX
    k2 = jnp.arange(half, dtype=jnp.int32)[:, None]
    n1 = jnp.arange(FFT_RADIX, dtype=jnp.int32)[None, :]
    n2 = jnp.arange(half, dtype=jnp.int32)[None, :]
    ang1 = ((k2 * n1) % n).astype(F32) * (2.0 * math.pi / n)
    ang2 = ((k2 * n2) % half).astype(F32) * (2.0 * math.pi / half)
    c1, s1 = jnp.cos(ang1).T[:, :, None], jnp.sin(ang1).T[:, :, None]
    c2, s2 = jnp.cos(ang2)[None], jnp.sin(ang2)[None]
    scale = n ** -0.5
    cm = (c1 * c2 - s1 * s2) * scale
    sm = (s1 * c2 + c1 * s2) * scale
    wp = jnp.concatenate([cm, -sm], axis=1).astype(BF16)
    wq = jnp.concatenate([-sm, -cm], axis=1).astype(BF16)
    return wp, wq


def _fourier_positions_fft(pd, qd, b_f, sgf):
    bsz, radix, half, width = pd.shape
    n = radix * half
    wp, wq = _fft_matrices(n)
    tc = 2 * LANES
    data = pl.BlockSpec((1, radix, half, tc), lambda b, c: (b, 0, 0, c))
    mats = pl.BlockSpec((radix, 2 * half, half), lambda b, c: (0, 0, 0))
    out = pl.pallas_call(
        _fft_body,
        grid=(bsz, width // tc),
        in_specs=[data, data, mats, mats,
                  pl.BlockSpec((1, tc), lambda b, c: (0, c)),
                  data],
        out_specs=data,
        out_shape=jax.ShapeDtypeStruct((bsz, radix, half, width), BF16),
        scratch_shapes=[pltpu.VMEM((radix, half, tc), F32), pltpu.VMEM((radix, half, tc), F32)],
        compiler_params=_params("parallel", "arbitrary"),
        name="fourier_positions_fft",
    )(pd, qd, wp, wq, b_f.reshape(1, width), sgf.reshape(bsz, radix, half, width))
    return out.reshape(bsz, n, width)


def _l0_out_body(yf_ref, yb_ref, sg_ref, fm_ref, w_ref, g_ref, gt_ref, x_ref, o_ref):
    width = yf_ref.shape[-1]
    r = (yf_ref[0].astype(F32) + yb_ref[0].astype(F32)) * sg_ref[0].astype(F32)
    y = (jnp.dot(r.astype(BF16), w_ref[:width, :], preferred_element_type=F32)
         + jnp.dot(fm_ref[0], w_ref[width:, :], preferred_element_type=F32))
    o_ref[0] = x_ref[0] + gt_ref[0] * _rms(y, g_ref[...])


def _l0_out_proj(yf, yb, sg, fmix, w_out, g_post, gt, x, per_batch_mod):
    bsz, n, width = yf.shape
    d = x.shape[-1]
    t = _row_tile(n, 512)
    mod_idx = (lambda b, j: (b, 0, 0)) if per_batch_mod else (lambda b, j: (0, 0, 0))
    row = lambda b, j: (b, j, 0)
    return pl.pallas_call(
        _l0_out_body,
        grid=(bsz, n // t),
        in_specs=[pl.BlockSpec((1, t, width), row)] * 4
                 + [pl.BlockSpec(w_out.shape, lambda b, j: (0, 0)),
                    pl.BlockSpec((1, d), lambda b, j: (0, 0)),
                    pl.BlockSpec((1, 1, d), mod_idx),
                    pl.BlockSpec((1, t, d), row)],
        out_specs=pl.BlockSpec((1, t, d), row),
        out_shape=jax.ShapeDtypeStruct((bsz, n, d), F32),
        compiler_params=_params("parallel", "arbitrary"),
        name="l0_out_proj",
    )(yf, yb, sg, fmix, w_out, g_post.reshape(1, d), gt, x)


def _rope(z, cos, sin_signed):
    quarter = HEAD_DIM // 4
    lane = lax.broadcasted_iota(jnp.int32, (z.shape[0], LANES), 1)
    first = (lane % (2 * quarter)) < quarter
    chunks = []
    for c in range(z.shape[1] // LANES):
        zc = z[:, c * LANES:(c + 1) * LANES]
        swapped = jnp.where(first, pltpu.roll(zc, LANES - quarter, 1), pltpu.roll(zc, quarter, 1))
        chunks.append(zc * cos + swapped * sin_signed)
    return jnp.concatenate(chunks, axis=1)


def _l1_in_body(x_ref, sc_ref, sh_ref, g_ref, w_ref, cos_ref, sin_ref, q_ref, k_ref, v_ref, sg_ref):
    qw, kw = q_ref.shape[-1], k_ref.shape[-1]
    h = _rms(x_ref[0], g_ref[...]) * (1.0 + sc_ref[0]) + sh_ref[0]
    hb = h.astype(BF16)
    cos, sin = cos_ref[...], sin_ref[...]
    zq = jnp.dot(hb, w_ref[:, :qw], preferred_element_type=F32)
    q_ref[0] = (_rope(zq, cos, sin) * (HEAD_DIM ** -0.5 * LOG2E)).astype(q_ref.dtype)
    zk = jnp.dot(hb, w_ref[:, qw:qw + kw], preferred_element_type=F32)
    k_ref[0] = _rope(zk, cos, sin).astype(k_ref.dtype)
    zv = jnp.dot(hb, w_ref[:, qw + kw:qw + 2 * kw], preferred_element_type=F32)
    v_ref[0] = zv.astype(v_ref.dtype)
    zg = jnp.dot(hb, w_ref[:, qw + 2 * kw:], preferred_element_type=F32)
    sg_ref[0] = _silu_of_double(zg).astype(sg_ref.dtype)


def _rope_tables(n):
    t = jnp.arange(n)
    row = (t // GRID_W).astype(F32)
    col = (t % GRID_W).astype(F32)
    half = HEAD_DIM // 2
    freqs = ROPE_BASE ** (-jnp.arange(0, half, 2, dtype=F32) / half)
    ar = row[:, None] * freqs[None, :]
    ac = col[:, None] * freqs[None, :]
    cos = jnp.concatenate([jnp.cos(ar), jnp.cos(ar), jnp.cos(ac), jnp.cos(ac)], axis=1)
    sin = jnp.concatenate([-jnp.sin(ar), jnp.sin(ar), -jnp.sin(ac), jnp.sin(ac)], axis=1)
    reps = LANES // HEAD_DIM
    return jnp.tile(cos, (1, reps)), jnp.tile(sin, (1, reps))


def _l1_in_proj(x, sc, sh, g_pre, w_in, q_width, kv_width):
    bsz, n, d = x.shape
    t = _row_tile(n, 512)
    cos, sin = _rope_tables(n)
    row = lambda b, j: (b, j, 0)
    mod_idx = lambda b, j: (b, 0, 0)
    return pl.pallas_call(
        _l1_in_body,
        grid=(bsz, n // t),
        in_specs=[pl.BlockSpec((1, t, d), row),
                  pl.BlockSpec((1, 1, d), mod_idx),
                  pl.BlockSpec((1, 1, d), mod_idx),
                  pl.BlockSpec((1, d), lambda b, j: (0, 0)),
                  pl.BlockSpec(w_in.shape, lambda b, j: (0, 0)),
                  pl.BlockSpec((t, LANES), lambda b, j: (j, 0)),
                  pl.BlockSpec((t, LANES), lambda b, j: (j, 0))],
        out_specs=[pl.BlockSpec((1, t, q_width), row),
                   pl.BlockSpec((1, t, kv_width), row),
                   pl.BlockSpec((1, t, kv_width), row),
                   pl.BlockSpec((1, t, q_width), row)],
        out_shape=[jax.ShapeDtypeStruct((bsz, n, q_width), BF16),
                   jax.ShapeDtypeStruct((bsz, n, kv_width), BF16),
                   jax.ShapeDtypeStruct((bsz, n, kv_width), BF16),
                   jax.ShapeDtypeStruct((bsz, n, q_width), BF16)],
        compiler_params=_params("parallel", "arbitrary"),
        name="l1_in_proj",
    )(x, sc, sh, g_pre.reshape(1, d), w_in, cos, sin)


def _ctx_kv_body(x_ref, sc_ref, sh_ref, g_ref, w_ref, k_ref, v_ref):
    kw = k_ref.shape[-1]
    h = _rms(x_ref[0], g_ref[...]) * (1.0 + sc_ref[0]) + sh_ref[0]
    z = jnp.dot(h.astype(BF16), w_ref[...], preferred_element_type=F32)
    k_ref[0] = z[:, :kw].astype(k_ref.dtype)
    v_ref[0] = z[:, kw:].astype(v_ref.dtype)


def _ctx_kv_proj(ctx, sc, sh, g_pre, w_kv):
    bsz, n, d = ctx.shape
    kw = w_kv.shape[1] // 2
    row = lambda b: (b, 0, 0)
    shared = lambda b: (0, 0, 0)
    out = jax.ShapeDtypeStruct((bsz, n, kw), BF16)
    return pl.pallas_call(
        _ctx_kv_body,
        grid=(bsz,),
        in_specs=[pl.BlockSpec((1, n, d), row),
                  pl.BlockSpec((1, 1, d), shared),
                  pl.BlockSpec((1, 1, d), shared),
                  pl.BlockSpec((1, d), lambda b: (0, 0)),
                  pl.BlockSpec(w_kv.shape, lambda b: (0, 0))],
        out_specs=[pl.BlockSpec((1, n, kw), row)] * 2,
        out_shape=[out, out],
        compiler_params=_params("arbitrary"),
        name="ctx_kv_proj",
    )(ctx, sc, sh, g_pre.reshape(1, d), w_kv)


def _attn_body(sink_ref, q_ref, km_ref, kp_ref, kn_ref, vm_ref, vp_ref, vn_ref, kc_ref, vc_ref,
               sg_ref, w_ref, g_ref, gt_ref, x_ref, o_ref, kf_ref, vf_ref, att_ref):
    i = pl.program_id(1)
    ni = pl.num_programs(1)
    tq = q_ref.shape[1]
    blk = ATTN_BLOCK
    per_step = tq // blk
    group = q_ref.shape[-1] // HEAD_DIM // N_KV_HEADS

    kf_ref[0:blk, :] = kp_ref[0]
    kf_ref[blk:blk + tq, :] = km_ref[0]
    kf_ref[blk + tq:, :] = kn_ref[0]
    vf_ref[0:blk, :] = vp_ref[0]
    vf_ref[blk:blk + tq, :] = vm_ref[0]
    vf_ref[blk + tq:, :] = vn_ref[0]

    ri = lax.broadcasted_iota(jnp.int32, (blk, blk), 0)
    cj = lax.broadcasted_iota(jnp.int32, (blk, blk), 1)
    tri_prev = jnp.where(cj >= ri, 0.0, NEG_INF)
    tri_next = jnp.where(cj <= ri, 0.0, NEG_INF)
    nt_dims = (((1,), (1,)), ((), ()))

    for qb in range(per_step):
        r0 = qb * blk
        prev_bias = tri_prev + jnp.where((i * per_step + qb) > 0, 0.0, NEG_INF)
        next_bias = tri_next + jnp.where((i * per_step + qb) < ni * per_step - 1, 0.0, NEG_INF)
        for kh in range(N_KV_HEADS):
            hs = slice(kh * HEAD_DIM, (kh + 1) * HEAD_DIM)
            q4 = jnp.concatenate(
                [q_ref[0, r0:r0 + blk, (kh * group + g) * HEAD_DIM:(kh * group + g + 1) * HEAD_DIM]
                 for g in range(group)], axis=0)
            s_lat = lax.dot_general(q4, kf_ref[r0:r0 + 3 * blk, hs], nt_dims,
                                    preferred_element_type=F32)
            s_ctx = lax.dot_general(q4, kc_ref[0, :, hs], nt_dims, preferred_element_type=F32)
            v_lat = vf_ref[r0:r0 + 3 * blk, hs]
            v_ctx = vc_ref[0, :, hs]
            outs = []
            for g in range(group):
                rows = slice(g * blk, (g + 1) * blk)
                sink = sink_ref[kh * group + g]
                cols = [s_lat[rows, :blk] + prev_bias, s_lat[rows, blk:2 * blk],
                        s_lat[rows, 2 * blk:] + next_bias]
                cols += [s_ctx[rows, c * blk:(c + 1) * blk] for c in range(s_ctx.shape[1] // blk)]
                mx = functools.reduce(jnp.maximum, cols)
                m = jnp.maximum(jnp.max(mx, axis=-1, keepdims=True), sink)
                e = [jnp.exp2(c - m) for c in cols]
                denom = (jnp.sum(functools.reduce(jnp.add, e), axis=-1, keepdims=True)
                         + jnp.exp2(sink - m))
                og = (jnp.dot(jnp.concatenate(e[:3], axis=1).astype(BF16), v_lat,
                              preferred_element_type=F32)
                      + jnp.dot(jnp.concatenate(e[3:], axis=1).astype(BF16), v_ctx,
                                preferred_element_type=F32))
                outs.append(og / denom)
            att_ref[r0:r0 + blk, kh * group * HEAD_DIM:(kh + 1) * group * HEAD_DIM] = (
                jnp.concatenate(outs, axis=1))

    gated = (att_ref[...] * sg_ref[0].astype(F32)).astype(BF16)
    y = jnp.dot(gated, w_ref[...], preferred_element_type=F32)
    o_ref[0] = x_ref[0] + gt_ref[0] * _rms(y, g_ref[...])


def _attention_out(q, k, v, kc, vc, sg, sink, w_out, g_post, gt, x):
    bsz, n, qw = q.shape
    kw = k.shape[-1]
    nc = kc.shape[1]
    d = x.shape[-1]
    blk = ATTN_BLOCK
    tq = _row_tile(n, 512)
    per = tq // blk
    nb = n // blk
    row = lambda b, i: (b, i, 0)
    prev = lambda b, i: (b, jnp.maximum(i * per - 1, 0), 0)
    nxt = lambda b, i: (b, jnp.minimum((i + 1) * per, nb - 1), 0)
    batch = lambda b, i: (b, 0, 0)
    kv_specs = [pl.BlockSpec((1, tq, kw), row), pl.BlockSpec((1, blk, kw), prev),
                pl.BlockSpec((1, blk, kw), nxt)]
    return pl.pallas_call(
        _attn_body,
        grid=(bsz, n // tq),
        in_specs=[pl.BlockSpec(memory_space=pltpu.SMEM),
                  pl.BlockSpec((1, tq, qw), row)]
                 + kv_specs + kv_specs
                 + [pl.BlockSpec((1, nc, kw), batch), pl.BlockSpec((1, nc, kw), batch),
                    pl.BlockSpec((1, tq, qw), row),
                    pl.BlockSpec(w_out.shape, lambda b, i: (0, 0)),
                    pl.BlockSpec((1, d), lambda b, i: (0, 0)),
                    pl.BlockSpec((1, 1, d), batch),
                    pl.BlockSpec((1, tq, d), row)],
        out_specs=pl.BlockSpec((1, tq, d), row),
        out_shape=jax.ShapeDtypeStruct((bsz, n, d), F32),
        scratch_shapes=[pltpu.VMEM((tq + 2 * blk, kw), BF16),
                        pltpu.VMEM((tq + 2 * blk, kw), BF16),
                        pltpu.VMEM((tq, qw), F32)],
        compiler_params=_params("parallel", "arbitrary"),
        name="window_attention_out",
    )(sink, q, k, k, k, v, v, v, kc, vc, sg, w_out, g_post.reshape(1, d), gt, x)


def _mod_vectors(c, c_ctx, w_mod, b_mod):
    bsz, d = c.shape
    rows = -(-(bsz + 1) // SUBLANES) * SUBLANES
    cvec = jnp.zeros((rows, d), F32).at[:bsz].set(c).at[bsz].set(c_ctx)
    m = _modulation(cvec, w_mod, b_mod)
    lat = [m[:bsz, k * d:(k + 1) * d].reshape(bsz, 1, d) for k in range(3)]
    cx = [m[bsz:bsz + 1, k * d:(k + 1) * d].reshape(1, 1, d) for k in range(3)]
    return lat, cx


def kernel(x, c, ctx, c_ctx, l0_w_mod, l0_b_mod, l0_g_pre, l0_g_post, l0_w_in, l0_w_conv, l0_b_conv,
           l0_w_a, l0_b_a, l0_w_x, l0_b_x, l0_lam, l0_w_f, l0_b_f, l0_w_out,
           l1_w_mod, l1_b_mod, l1_g_pre, l1_g_post, l1_w_in, l1_sink, l1_w_out):
    bsz, n, d = x.shape
    lru_w = l0_w_conv.shape[1]
    f_w = l0_w_f.shape[0] * l0_w_f.shape[1]
    assert lru_w == f_w == d

    (sh_l, sc_l, gt_l), (sh_c, sc_c, gt_c) = _mod_vectors(c, c_ctx, l0_w_mod, l0_b_mod)
    wp, wq = _fold_fourier_weights(l0_w_in, l0_w_f, 2 * lru_w)
    w_cat = jnp.concatenate([l0_w_in[:, :lru_w], 0.5 * l0_w_in[:, lru_w:2 * lru_w], wp, wq,
                             0.5 * l0_w_in[:, 2 * lru_w + f_w:]], axis=1).astype(BF16)
    wg = _gate_weights(l0_w_a, l0_w_x)
    w_out0 = l0_w_out.astype(BF16)

    use_fft = n % (FFT_RADIX * 32) == 0 and n >= 1024
    conv = (l0_w_conv, l0_b_conv)
    u_c, sg_c, p_c, q_c, sgf_c = _l0_in_proj(ctx, sc_c, sh_c, l0_g_pre, w_cat, *conv, False, False)
    u_l, sg_l, p_l, q_l, sgf_l = _l0_in_proj(x, sc_l, sh_l, l0_g_pre, w_cat, *conv, True, use_fft)

    zeros = jnp.zeros((bsz, 1, lru_w), F32)
    lru_args = (wg, l0_b_a, l0_b_x, l0_lam)
    yf_c, yb_c, hf_c, hb_c = _rglru(u_c, *lru_args, zeros, zeros)
    yf_l, yb_l, _, _ = _rglru(u_l, *lru_args, hf_c, hb_c)

    fm_c = _fourier_positions(p_c, q_c, l0_b_f, sgf_c)
    fm_l = (_fourier_positions_fft if use_fft else _fourier_positions)(p_l, q_l, l0_b_f, sgf_l)

    ctx1 = _l0_out_proj(yf_c, yb_c, sg_c, fm_c, w_out0, l0_g_post, gt_c, ctx, False)
    x1 = _l0_out_proj(yf_l, yb_l, sg_l, fm_l, w_out0, l0_g_post, gt_l, x, True)

    (sh_l, sc_l, gt_l), (sh_c, sc_c, _) = _mod_vectors(c, c_ctx, l1_w_mod, l1_b_mod)
    q_width = l1_w_out.shape[0]
    kv_width = (l1_w_in.shape[1] - 2 * q_width) // 2
    w_in1 = jnp.concatenate([l1_w_in[:, :q_width + 2 * kv_width],
                             0.5 * l1_w_in[:, q_width + 2 * kv_width:]], axis=1).astype(BF16)
    kc, vc = _ctx_kv_proj(ctx1, sc_c, sh_c, l1_g_pre, w_in1[:, q_width:q_width + 2 * kv_width])
    q, k, v, sg1 = _l1_in_proj(x1, sc_l, sh_l, l1_g_pre, w_in1, q_width, kv_width)
    return _attention_out(q, k, v, kc, vc, sg1, l1_sink * LOG2E, l1_w_out.astype(BF16), l1_g_post,
                          gt_l, x1)
```

```python
import functools
import math

import jax
import jax.numpy as jnp
import numpy as np
from jax import lax
from jax.experimental import pallas as pl
from jax.experimental.pallas import tpu as pltpu

EPS = 1e-6
NEG_INF = -1e30
LRU_HEADS = 16
LRU_C = 8.0
CONV_W = 4
CONV_LEFT = 2
FNET_GROUPS = 8
HEAD_DIM = 64
N_KV_HEADS = 4
WINDOW = 128
ATTN_BLOCK = 128
GRID_W = 64
ROPE_BASE = 10000.0
LOG2E = math.log2(math.e)

LANES = 128
SUBLANES = 8
BF16_ROWS = 16
FFT_RADIX = 16
GATE_COLS = 256
SCAN_SEG = 4
VMEM_LIMIT = 56 * 1024 * 1024

BF16 = jnp.bfloat16
F32 = jnp.float32


def _params(*sem):
    return pltpu.CompilerParams(dimension_semantics=sem, vmem_limit_bytes=VMEM_LIMIT)


def _sigmoid(z):
    return 0.5 * jnp.tanh(0.5 * z) + 0.5


def _silu(z):
    return z * _sigmoid(z)


def _silu_of_double(zh):
    return zh * jnp.tanh(zh) + zh


def _rms(v, g):
    return v * lax.rsqrt(jnp.mean(v * v, axis=-1, keepdims=True) + EPS) * g


def _row_tile(n, want):
    t = min(n, want)
    assert n % t == 0
    return t


def _mod_body(c_ref, w_ref, b_ref, o_ref):
    o_ref[...] = jnp.dot(_silu(c_ref[...]), w_ref[...], preferred_element_type=F32,
                         precision=lax.Precision.HIGHEST) + b_ref[...]


def _modulation(cvec, w_mod, b_mod):
    rows, d = cvec.shape
    width = w_mod.shape[1]
    tn = _row_tile(width, 1024)
    return pl.pallas_call(
        _mod_body,
        grid=(width // tn,),
        in_specs=[pl.BlockSpec((rows, d), lambda j: (0, 0)),
                  pl.BlockSpec((d, tn), lambda j: (0, j)),
                  pl.BlockSpec((1, tn), lambda j: (0, j))],
        out_specs=pl.BlockSpec((rows, tn), lambda j: (0, j)),
        out_shape=jax.ShapeDtypeStruct((rows, width), F32),
        compiler_params=_params("arbitrary"),
        name="modulation",
    )(cvec, w_mod, b_mod.reshape(1, width))


def _fold_body(win_ref, wf_ref, cd_ref, sd_ref, wp_ref, wq_ref):
    hi = lax.Precision.HIGHEST
    win = win_ref[...]
    wf = wf_ref[0]
    cw = jnp.dot(cd_ref[...], wf, preferred_element_type=F32, precision=hi)
    sw = jnp.dot(sd_ref[...], wf, preferred_element_type=F32, precision=hi)
    wp_ref[...] = jnp.dot(win, cw, preferred_element_type=F32, precision=hi)
    wq_ref[...] = jnp.dot(win, sw, preferred_element_type=F32, precision=hi)


def _fold_fourier_weights(w_in, w_f, col0):
    d = w_in.shape[0]
    groups, gd, _ = w_f.shape
    idx = np.arange(gd)
    ang = 2.0 * np.pi * ((idx[:, None] * idx[None, :]) % gd) / gd
    cd = jnp.asarray(np.cos(ang) / math.sqrt(gd), F32)
    sd = jnp.asarray(np.sin(ang) / math.sqrt(gd), F32)
    out = jax.ShapeDtypeStruct((d, groups * gd), F32)
    return pl.pallas_call(
        _fold_body,
        grid=(groups,),
        in_specs=[pl.BlockSpec((d, gd), lambda g: (0, col0 // gd + g)),
                  pl.BlockSpec((1, gd, gd), lambda g: (g, 0, 0)),
                  pl.BlockSpec((gd, gd), lambda g: (0, 0)),
                  pl.BlockSpec((gd, gd), lambda g: (0, 0))],
        out_specs=[pl.BlockSpec((d, gd), lambda g: (0, g)),
                   pl.BlockSpec((d, gd), lambda g: (0, g))],
        out_shape=[out, out],
        compiler_params=_params("arbitrary"),
        name="fold_fourier_weights",
    )(w_in, w_f, cd, sd)


def _l0_in_body(decimate, x_ref, xp_ref, xn_ref, sc_ref, sh_ref, g_ref, w_ref, wconv_ref, bconv_ref,
                u_ref, sg_ref, p_ref, q_ref, sgf_ref, *slab):
    j = pl.program_id(1)
    width = u_ref.shape[-1]
    t = x_ref.shape[1]

    def modulated(v):
        return _rms(v, g_ref[...]) * (1.0 + sc_ref[0]) + sh_ref[0]

    h = modulated(x_ref[0])
    hb = h.astype(BF16)

    h_prev = modulated(xp_ref[0]) * jnp.where(j > 0, 1.0, 0.0)
    h_next = modulated(xn_ref[0]) * jnp.where(j < pl.num_programs(1) - 1, 1.0, 0.0)
    h_ext = jnp.concatenate([h_prev, h, h_next], axis=0).astype(BF16)
    z_ext = jnp.dot(h_ext, w_ref[:, :width], preferred_element_type=F32)
    uc = bconv_ref[...] + jnp.zeros((t, width), F32)
    for k in range(CONV_W):
        shift = (CONV_LEFT - k) % (t + 2 * SUBLANES)
        tap = z_ext if shift == 0 else pltpu.roll(z_ext, shift, 0)
        uc = uc + tap[SUBLANES:SUBLANES + t] * wconv_ref[k:k + 1, :]
    u_ref[0] = uc.astype(u_ref.dtype)

    hb_dec = hb
    if decimate:
        slab_ref, = slab
        per = p_ref.shape[2]
        d = h.shape[-1]
        for c in range(d // LANES):
            slab_ref[c] = h[:, c * LANES:(c + 1) * LANES]
        hb_dec = jnp.concatenate(
            [jnp.concatenate([slab_ref[c, pl.ds(n1, per, stride=FFT_RADIX), :]
                              for c in range(d // LANES)], axis=1)
             for n1 in range(FFT_RADIX)], axis=0).astype(BF16)
    outs = (None, sg_ref, p_ref, q_ref, sgf_ref)
    gated = (False, True, False, False, True)
    for part, (o_ref, is_gate) in enumerate(zip(outs, gated)):
        if o_ref is None:
            continue
        lhs = hb_dec if part in (2, 3) else hb
        z = jnp.dot(lhs, w_ref[:, part * width:(part + 1) * width], preferred_element_type=F32)
        if is_gate:
            z = _silu_of_double(z)
        o_ref[...] = z.astype(o_ref.dtype).reshape(o_ref.shape)


def _l0_in_proj(x, sc, sh, g_pre, w_cat, w_conv, b_conv, per_batch_mod, decimate):
    bsz, n, d = x.shape
    width = w_cat.shape[1] // 5
    t = _row_tile(n, 512)
    per8 = t // SUBLANES
    last8 = n // SUBLANES - 1
    mod_idx = (lambda b, j: (b, 0, 0)) if per_batch_mod else (lambda b, j: (0, 0, 0))
    out = jax.ShapeDtypeStruct((bsz, n, width), BF16)
    row_spec = pl.BlockSpec((1, t, width), lambda b, j: (b, j, 0))
    if decimate:
        dec = jax.ShapeDtypeStruct((bsz, FFT_RADIX, n // FFT_RADIX, width), BF16)
        dec_spec = pl.BlockSpec((1, FFT_RADIX, t // FFT_RADIX, width), lambda b, j: (b, 0, j, 0))
        out_specs = [row_spec, row_spec, dec_spec, dec_spec, row_spec]
        out_shape = [out, out, dec, dec, out]
        scratch = [pltpu.VMEM((d // LANES, t, LANES), F32)]
    else:
        out_specs, out_shape, scratch = [row_spec] * 5, [out] * 5, []
    return pl.pallas_call(
        functools.partial(_l0_in_body, decimate),
        grid=(bsz, n // t),
        in_specs=[pl.BlockSpec((1, t, d), lambda b, j: (b, j, 0)),
                  pl.BlockSpec((1, SUBLANES, d), lambda b, j: (b, jnp.maximum(j * per8 - 1, 0), 0)),
                  pl.BlockSpec((1, SUBLANES, d),
                               lambda b, j: (b, jnp.minimum((j + 1) * per8, last8), 0)),
                  pl.BlockSpec((1, 1, d), mod_idx),
                  pl.BlockSpec((1, 1, d), mod_idx),
                  pl.BlockSpec((1, d), lambda b, j: (0, 0)),
                  pl.BlockSpec(w_cat.shape, lambda b, j: (0, 0)),
                  pl.BlockSpec(w_conv.shape, lambda b, j: (0, 0)),
                  pl.BlockSpec((1, width), lambda b, j: (0, 0))],
        out_specs=out_specs,
        out_shape=out_shape,
        scratch_shapes=scratch,
        compiler_params=_params("parallel", "arbitrary"),
        name="l0_in_proj",
    )(x, x, x, sc, sh, g_pre.reshape(1, d), w_cat, w_conv, b_conv.reshape(1, width))


def _lru_direction(rev, scan_start, u_ref, wg_ref, ba_ref, bx_ref, lam_ref, h0_ref, y_ref, hfin_ref,
                   a_ref, b_ref, carry_ref):
    _, t, width = u_ref.shape
    d = 1 if rev else 0

    z = -lam_ref[d:d + 1, :]
    c_half = (-0.5 * LRU_C) * (jnp.maximum(z, 0.0) + jnp.log1p(jnp.exp(-jnp.abs(z))))
    ba_half = 0.5 * ba_ref[d:d + 1, :]
    bx_half = 0.5 * bx_ref[d:d + 1, :]

    for cg in range(width // GATE_COLS):
        cols = slice(cg * GATE_COLS, (cg + 1) * GATE_COLS)
        ub = u_ref[0, :, cols]
        zg = jnp.dot(ub, wg_ref[d, cg], preferred_element_type=F32)
        ta = jnp.tanh(zg[:, :GATE_COLS] + ba_half[:, cols])
        ti = jnp.tanh(zg[:, GATE_COLS:] + bx_half[:, cols])
        log_a = c_half[:, cols] * ta + c_half[:, cols]
        a = jnp.exp(log_a)
        v = -jnp.tanh(log_a) * (1.0 + a * a)
        amp = jnp.where(v > 0.0, v * lax.rsqrt(v), 0.0)
        u_half = 0.5 * ub.astype(F32)
        bt = amp * (u_half * ti + u_half)
        for piece in range(GATE_COLS // LANES):
            lanes = slice(piece * LANES, (piece + 1) * LANES)
            a_ref[cg * (GATE_COLS // LANES) + piece] = a[:, lanes]
            b_ref[cg * (GATE_COLS // LANES) + piece] = bt[:, lanes]

    @pl.when(scan_start)
    def _():
        carry_ref[d:d + 1, :] = h0_ref[0]

    slabs = width // LANES
    blk = SCAN_SEG * SUBLANES
    nblk = t // blk
    row = lax.broadcasted_iota(jnp.int32, (SUBLANES, LANES), 0)
    order = tuple(range(SCAN_SEG - 1, -1, -1) if rev else range(SCAN_SEG))

    def body(it, hs):
        g = (nblk - 1 - it) if rev else it
        base = pl.multiple_of(g * blk, blk)
        out = []
        for c in range(slabs):
            h_in = hs[c]
            av = [a_ref[c, pl.ds(base + i, SUBLANES, stride=SCAN_SEG), :] for i in range(SCAN_SEG)]
            bv = [b_ref[c, pl.ds(base + i, SUBLANES, stride=SCAN_SEG), :] for i in range(SCAN_SEG)]
            prod, acc = {order[0]: av[order[0]]}, {order[0]: bv[order[0]]}
            for before, i in zip(order[:-1], order[1:]):
                acc[i] = av[i] * acc[before] + bv[i]
                prod[i] = av[i] * prod[before]
            pc, hc = prod[order[-1]], acc[order[-1]]
            for s in (1, 2, 4):
                if rev:
                    shift, m = SUBLANES - s, row < SUBLANES - s
                else:
                    shift, m = s, row >= s
                p_sh = pltpu.roll(pc, shift, 0)
                h_sh = pltpu.roll(hc, shift, 0)
                hc = jnp.where(m, pc * h_sh + hc, hc)
                pc = jnp.where(m, pc * p_sh, pc)
            h_end = hc + pc * h_in
            if rev:
                h_start = jnp.where(row < SUBLANES - 1, pltpu.roll(h_end, SUBLANES - 1, 0), h_in)
            else:
                h_start = jnp.where(row >= 1, pltpu.roll(h_end, 1, 0), h_in)
            for i in range(SCAN_SEG):
                b_ref[c, pl.ds(base + i, SUBLANES, stride=SCAN_SEG), :] = acc[i] + prod[i] * h_start
            out.append(h_end[0:1, :] if rev else h_end[SUBLANES - 1:SUBLANES, :])
        return tuple(out)

    init = tuple(carry_ref[d:d + 1, c * LANES:(c + 1) * LANES] for c in range(slabs))
    h_last = jnp.concatenate(lax.fori_loop(0, nblk, body, init, unroll=2), axis=1)
    carry_ref[d:d + 1, :] = h_last
    hfin_ref[0] = h_last
    for c in range(slabs):
        y_ref[0, :, c * LANES:(c + 1) * LANES] = b_ref[c].astype(y_ref.dtype)


def _lru_body(uf_ref, ub_ref, wg_ref, ba_ref, bx_ref, lam_ref, h0f_ref, h0b_ref,
              yf_ref, yb_ref, hf_ref, hb_ref, a_ref, b_ref, carry_ref):
    first = pl.program_id(1) == 0
    common = (wg_ref, ba_ref, bx_ref, lam_ref)
    scratch = (a_ref, b_ref, carry_ref)
    _lru_direction(False, first, uf_ref, *common, h0f_ref, yf_ref, hf_ref, *scratch)
    _lru_direction(True, first, ub_ref, *common, h0b_ref, yb_ref, hb_ref, *scratch)


def _rglru(u, wg, b_a, b_x, lam, h0f, h0b):
    bsz, n, width = u.shape
    t = _row_tile(n, 512)
    nt = n // t
    fwd = lambda b, j: (b, j, 0)
    bwd = lambda b, j: (b, nt - 1 - j, 0)
    tile = (1, t, width)
    const2 = lambda b, j: (0, 0)
    state = pl.BlockSpec((1, 1, width), lambda b, j: (b, 0, 0))
    y_shape = jax.ShapeDtypeStruct((bsz, n, width), BF16)
    h_shape = jax.ShapeDtypeStruct((bsz, 1, width), F32)
    return pl.pallas_call(
        _lru_body,
        grid=(bsz, nt),
        in_specs=[pl.BlockSpec(tile, fwd), pl.BlockSpec(tile, bwd),
                  pl.BlockSpec(wg.shape, lambda b, j: (0, 0, 0, 0)),
                  pl.BlockSpec(b_a.shape, const2),
                  pl.BlockSpec(b_x.shape, const2),
                  pl.BlockSpec(lam.shape, const2),
                  state, state],
        out_specs=[pl.BlockSpec(tile, fwd), pl.BlockSpec(tile, bwd), state, state],
        out_shape=[y_shape, y_shape, h_shape, h_shape],
        scratch_shapes=[pltpu.VMEM((width // LANES, t, LANES), F32),
                        pltpu.VMEM((width // LANES, t, LANES), F32),
                        pltpu.VMEM((2, width), F32)],
        compiler_params=_params("parallel", "arbitrary"),
        name="rglru_scan",
    )(u, u, wg, b_a, b_x, lam, h0f, h0b)


def _gate_weights(w_a, w_x):
    _, heads, blk, _ = w_a.shape
    per = GATE_COLS // blk
    groups = heads // per
    eye = jnp.eye(per, dtype=w_a.dtype)

    def bd(w):
        w = w.reshape(2, groups, per, blk, blk)
        full = jnp.einsum('dgpij,pq->dgpiqj', w, eye)
        return full.reshape(2, groups, GATE_COLS, GATE_COLS)

    return (0.5 * jnp.concatenate([bd(w_a), bd(w_x)], axis=-1)).astype(BF16)


def _dft_body(c_ref, s_ref, p_ref, q_ref, bf_ref, sgf_ref, o_ref, acc_ref):
    k = pl.program_id(2)

    @pl.when(k == 0)
    def _():
        acc_ref[...] = jnp.zeros_like(acc_ref)

    acc_ref[...] += (jnp.dot(c_ref[...], p_ref[0], preferred_element_type=F32)
                     + jnp.dot(s_ref[...], q_ref[0], preferred_element_type=F32))

    @pl.when(k == pl.num_programs(2) - 1)
    def _():
        o_ref[0] = ((acc_ref[...] + bf_ref[...]) * sgf_ref[0].astype(F32)).astype(o_ref.dtype)


def _dft_matrices(n):
    n1 = 1
    while n1 * n1 < n:
        n1 *= 2
    n2 = n // n1
    m = jnp.arange(n, dtype=jnp.int32)[None, :]
    hi = jnp.arange(n2, dtype=jnp.int32)[:, None]
    lo = jnp.arange(n1, dtype=jnp.int32)[:, None]
    ang_hi = ((hi * n1 * m) % n).astype(F32) * (2.0 * math.pi / n)
    ang_lo = ((lo * m) % n).astype(F32) * (2.0 * math.pi / n)
    ch, sh_ = jnp.cos(ang_hi)[:, None, :], jnp.sin(ang_hi)[:, None, :]
    cl, sl = jnp.cos(ang_lo)[None, :, :], jnp.sin(ang_lo)[None, :, :]
    scale = n ** -0.5
    cmat = ((ch * cl - sh_ * sl) * scale).reshape(n, n).astype(BF16)
    smat = ((sh_ * cl + ch * sl) * (-scale)).reshape(n, n).astype(BF16)
    return cmat, smat


def _fourier_positions(p, q, b_f, sgf):
    bsz, n, width = p.shape
    cmat, smat = _dft_matrices(n)
    tm = _row_tile(n, 1024)
    tk = _row_tile(n, 1024)
    return pl.pallas_call(
        _dft_body,
        grid=(bsz, n // tm, n // tk),
        in_specs=[pl.BlockSpec((tm, tk), lambda b, i, k: (i, k)),
                  pl.BlockSpec((tm, tk), lambda b, i, k: (i, k)),
                  pl.BlockSpec((1, tk, width), lambda b, i, k: (b, k, 0)),
                  pl.BlockSpec((1, tk, width), lambda b, i, k: (b, k, 0)),
                  pl.BlockSpec((1, width), lambda b, i, k: (0, 0)),
                  pl.BlockSpec((1, tm, width), lambda b, i, k: (b, i, 0))],
        out_specs=pl.BlockSpec((1, tm, width), lambda b, i, k: (b, i, 0)),
        out_shape=jax.ShapeDtypeStruct((bsz, n, width), BF16),
        scratch_shapes=[pltpu.VMEM((tm, width), F32)],
        compiler_params=_params("parallel", "parallel", "arbitrary"),
        name="fourier_positions",
    )(cmat, smat, p, q, b_f.reshape(1, width), sgf)


def _fft_real(re, im):
    root_half = math.sqrt(0.5)

    def rec(re, im, want_im):
        n = len(re)
        if n == 1:
            return list(re), list(im)
        er, ei = rec(re[0::2], im[0::2], True)
        xr, xi = rec(re[1::2], im[1::2], True)
        out_r, out_i = [None] * n, [None] * n
        for k in range(n // 2):
            if k == 0:
                tr, ti = xr[k], xi[k]
            elif 4 * k == n:
                tr, ti = xi[k], None
            elif 8 * k == n:
                tr = (xr[k] + xi[k]) * root_half
                ti = (xi[k] - xr[k]) * root_half if want_im else None
            elif 8 * k == 3 * n:
                tr = (xi[k] - xr[k]) * root_half
                ti = (xr[k] + xi[k]) * (-root_half) if want_im else None
            else:
                c, s = math.cos(2 * math.pi * k / n), math.sin(2 * math.pi * k / n)
                tr = xr[k] * c + xi[k] * s
                ti = xi[k] * c - xr[k] * s if want_im else None
            out_r[k] = er[k] + tr
            out_r[k + n // 2] = er[k] - tr
            if want_im:
                if 4 * k == n:
                    out_i[k] = ei[k] - xr[k]
                    out_i[k + n // 2] = ei[k] + xr[k]
                else:
                    out_i[k] = ei[k] + ti
                    out_i[k + n // 2] = ei[k] - ti
        return out_r, out_i

    return rec(list(re), list(im), False)[0]


def _fft_body(p_ref, q_ref, wp_ref, wq_ref, bf_ref, sgf_ref, o_ref, yre_ref, yim_ref):
    radix, half, tc = yre_ref.shape
    for n1 in range(radix):
        y = (jnp.dot(wp_ref[n1], p_ref[0, n1], preferred_element_type=F32)
             + jnp.dot(wq_ref[n1], q_ref[0, n1], preferred_element_type=F32))
        yre_ref[n1] = y[:half]
        yim_ref[n1] = y[half:]

    rows = BF16_ROWS

    def chunk(it, carry):
        r0 = pl.multiple_of(it * rows, rows)
        for lt in range(tc // LANES):
            ls = slice(lt * LANES, (lt + 1) * LANES)
            re = [yre_ref[n1, pl.ds(r0, rows), ls] for n1 in range(radix)]
            im = [yim_ref[n1, pl.ds(r0, rows), ls] for n1 in range(radix)]
            out = _fft_real(re, im)
            for k1 in range(radix):
                gate = sgf_ref[0, k1, pl.ds(r0, rows), ls].astype(F32)
                o_ref[0, k1, pl.ds(r0, rows), ls] = ((out[k1] + bf_ref[:, ls]) * gate).astype(o_ref.dtype)
        return carry

    lax.fori_loop(0, half // rows, chunk, 0)


def _fft_matrices(n):
    half = n // FFT_RADIX
    k2 = jnp.arange(half, dtype=jnp.int32)[:, None]
    n1 = jnp.arange(FFT_RADIX, dtype=jnp.int32)[None, :]
    n2 = jnp.arange(half, dtype=jnp.int32)[None, :]
    ang1 = ((k2 * n1) % n).astype(F32) * (2.0 * math.pi / n)
    ang2 = ((k2 * n2) % half).astype(F32) * (2.0 * math.pi / half)
    c1, s1 = jnp.cos(ang1).T[:, :, None], jnp.sin(ang1).T[:, :, None]
    c2, s2 = jnp.cos(ang2)[None], jnp.sin(ang2)[None]
    scale = n ** -0.5
    cm = (c1 * c2 - s1 * s2) * scale
    sm = (s1 * c2 + c1 * s2) * scale
    wp = jnp.concatenate([cm, -sm], axis=1).astype(BF16)
    wq = jnp.concatenate([-sm, -cm], axis=1).astype(BF16)
    return wp, wq


def _fourier_positions_fft(pd, qd, b_f, sgf):
    bsz, radix, half, width = pd.shape
    n = radix * half
    wp, wq = _fft_matrices(n)
    tc = 2 * LANES
    data = pl.BlockSpec((1, radix, half, tc), lambda b, c: (b, 0, 0, c))
    mats = pl.BlockSpec((radix, 2 * half, half), lambda b, c: (0, 0, 0))
    out = pl.pallas_call(
        _fft_body,
        grid=(bsz, width // tc),
        in_specs=[data, data, mats, mats,
                  pl.BlockSpec((1, tc), lambda b, c: (0, c)),
                  data],
        out_specs=data,
        out_shape=jax.ShapeDtypeStruct((bsz, radix, half, width), BF16),
        scratch_shapes=[pltpu.VMEM((radix, half, tc), F32), pltpu.VMEM((radix, half, tc), F32)],
        compiler_params=_params("parallel", "arbitrary"),
        name="fourier_positions_fft",
    )(pd, qd, wp, wq, b_f.reshape(1, width), sgf.reshape(bsz, radix, half, width))
    return out.reshape(bsz, n, width)


def _l0_out_body(yf_ref, yb_ref, sg_ref, fm_ref, w_ref, g_ref, gt_ref, x_ref, o_ref):
    width = yf_ref.shape[-1]
    r = (yf_ref[0].astype(F32) + yb_ref[0].astype(F32)) * sg_ref[0].astype(F32)
    y = (jnp.dot(r.astype(BF16), w_ref[:width, :], preferred_element_type=F32)
         + jnp.dot(fm_ref[0], w_ref[width:, :], preferred_element_type=F32))
    o_ref[0] = x_ref[0] + gt_ref[0] * _rms(y, g_ref[...])


def _l0_out_proj(yf, yb, sg, fmix, w_out, g_post, gt, x, per_batch_mod):
    bsz, n, width = yf.shape
    d = x.shape[-1]
    t = _row_tile(n, 512)
    mod_idx = (lambda b, j: (b, 0, 0)) if per_batch_mod else (lambda b, j: (0, 0, 0))
    row = lambda b, j: (b, j, 0)
    return pl.pallas_call(
        _l0_out_body,
        grid=(bsz, n // t),
        in_specs=[pl.BlockSpec((1, t, width), row)] * 4
                 + [pl.BlockSpec(w_out.shape, lambda b, j: (0, 0)),
                    pl.BlockSpec((1, d), lambda b, j: (0, 0)),
                    pl.BlockSpec((1, 1, d), mod_idx),
                    pl.BlockSpec((1, t, d), row)],
        out_specs=pl.BlockSpec((1, t, d), row),
        out_shape=jax.ShapeDtypeStruct((bsz, n, d), F32),
        compiler_params=_params("parallel", "arbitrary"),
        name="l0_out_proj",
    )(yf, yb, sg, fmix, w_out, g_post.reshape(1, d), gt, x)


def _rope(z, cos, sin_signed):
    quarter = HEAD_DIM // 4
    lane = lax.broadcasted_iota(jnp.int32, (z.shape[0], LANES), 1)
    first = (lane % (2 * quarter)) < quarter
    chunks = []
    for c in range(z.shape[1] // LANES):
        zc = z[:, c * LANES:(c + 1) * LANES]
        swapped = jnp.where(first, pltpu.roll(zc, LANES - quarter, 1), pltpu.roll(zc, quarter, 1))
        chunks.append(zc * cos + swapped * sin_signed)
    return jnp.concatenate(chunks, axis=1)


def _l0_out_l1_in_body(yf_ref, yb_ref, sg0_ref, fm_ref, w0_ref, g0_ref, gt0_ref, x_ref,
                       sc_ref, sh_ref, g_ref, w_ref, cos_ref, sin_ref,
                       x1_ref, q_ref, k_ref, v_ref, sg_ref):
    width = yf_ref.shape[-1]
    r = (yf_ref[0].astype(F32) + yb_ref[0].astype(F32)) * sg0_ref[0].astype(F32)
    y = (jnp.dot(r.astype(BF16), w0_ref[:width, :], preferred_element_type=F32)
         + jnp.dot(fm_ref[0], w0_ref[width:, :], preferred_element_type=F32))
    x1 = x_ref[0] + gt0_ref[0] * _rms(y, g0_ref[...])
    x1_ref[0] = x1

    qw, kw = q_ref.shape[-1], v_ref.shape[-1]
    h = _rms(x1, g_ref[...]) * (1.0 + sc_ref[0]) + sh_ref[0]
    hb = h.astype(BF16)
    cos, sin = cos_ref[...], sin_ref[...]
    zq = jnp.dot(hb, w_ref[:, :qw], preferred_element_type=F32)
    q_ref[0] = (_rope(zq, cos, sin) * (HEAD_DIM ** -0.5 * LOG2E)).astype(q_ref.dtype)
    zk = jnp.dot(hb, w_ref[:, qw:qw + kw], preferred_element_type=F32)
    k_ref[0] = _rope(zk, cos, sin).T.astype(k_ref.dtype)
    zv = jnp.dot(hb, w_ref[:, qw + kw:qw + 2 * kw], preferred_element_type=F32)
    v_ref[0] = zv.astype(v_ref.dtype)
    zg = jnp.dot(hb, w_ref[:, qw + 2 * kw:], preferred_element_type=F32)
    sg_ref[0] = _silu_of_double(zg).astype(sg_ref.dtype)


def _rope_tables(n):
    t = jnp.arange(n)
    row = (t // GRID_W).astype(F32)
    col = (t % GRID_W).astype(F32)
    half = HEAD_DIM // 2
    freqs = ROPE_BASE ** (-jnp.arange(0, half, 2, dtype=F32) / half)
    ar = row[:, None] * freqs[None, :]
    ac = col[:, None] * freqs[None, :]
    cos = jnp.concatenate([jnp.cos(ar), jnp.cos(ar), jnp.cos(ac), jnp.cos(ac)], axis=1)
    sin = jnp.concatenate([-jnp.sin(ar), jnp.sin(ar), -jnp.sin(ac), jnp.sin(ac)], axis=1)
    reps = LANES // HEAD_DIM
    return jnp.tile(cos, (1, reps)), jnp.tile(sin, (1, reps))


def _l0_out_l1_in(yf, yb, sg0, fmix, w_out0, g_post0, gt0, x, sc, sh, g_pre, w_in, q_width, kv_width):
    bsz, n, width = yf.shape
    d = x.shape[-1]
    t = _row_tile(n, 512)
    cos, sin = _rope_tables(n)
    row = lambda b, j: (b, j, 0)
    mod_idx = lambda b, j: (b, 0, 0)
    const2 = lambda b, j: (0, 0)
    return pl.pallas_call(
        _l0_out_l1_in_body,
        grid=(bsz, n // t),
        in_specs=[pl.BlockSpec((1, t, width), row)] * 4
                 + [pl.BlockSpec(w_out0.shape, const2),
                    pl.BlockSpec((1, d), const2),
                    pl.BlockSpec((1, 1, d), mod_idx),
                    pl.BlockSpec((1, t, d), row),
                    pl.BlockSpec((1, 1, d), mod_idx),
                    pl.BlockSpec((1, 1, d), mod_idx),
                    pl.BlockSpec((1, d), const2),
                    pl.BlockSpec(w_in.shape, const2),
                    pl.BlockSpec((t, LANES), lambda b, j: (j, 0)),
                    pl.BlockSpec((t, LANES), lambda b, j: (j, 0))],
        out_specs=[pl.BlockSpec((1, t, d), row),
                   pl.BlockSpec((1, t, q_width), row),
                   pl.BlockSpec((1, kv_width, t), lambda b, j: (b, 0, j)),
                   pl.BlockSpec((1, t, kv_width), row),
                   pl.BlockSpec((1, t, q_width), row)],
        out_shape=[jax.ShapeDtypeStruct((bsz, n, d), F32),
                   jax.ShapeDtypeStruct((bsz, n, q_width), BF16),
                   jax.ShapeDtypeStruct((bsz, kv_width, n), BF16),
                   jax.ShapeDtypeStruct((bsz, n, kv_width), BF16),
                   jax.ShapeDtypeStruct((bsz, n, q_width), BF16)],
        compiler_params=_params("parallel", "arbitrary"),
        name="l0_out_l1_in",
    )(yf, yb, sg0, fmix, w_out0, g_post0.reshape(1, d), gt0, x,
      sc, sh, g_pre.reshape(1, d), w_in, cos, sin)


def _ctx_kv_body(x_ref, sc_ref, sh_ref, g_ref, w_ref, k_ref, v_ref):
    kw = v_ref.shape[-1]
    h = _rms(x_ref[0], g_ref[...]) * (1.0 + sc_ref[0]) + sh_ref[0]
    z = jnp.dot(h.astype(BF16), w_ref[...], preferred_element_type=F32)
    k_ref[0] = z[:, :kw].T.astype(k_ref.dtype)
    v_ref[0] = z[:, kw:].astype(v_ref.dtype)


def _ctx_kv_proj(ctx, sc, sh, g_pre, w_kv):
    bsz, n, d = ctx.shape
    kw = w_kv.shape[1] // 2
    row = lambda b: (b, 0, 0)
    shared = lambda b: (0, 0, 0)
    out = jax.ShapeDtypeStruct((bsz, n, kw), BF16)
    return pl.pallas_call(
        _ctx_kv_body,
        grid=(bsz,),
        in_specs=[pl.BlockSpec((1, n, d), row),
                  pl.BlockSpec((1, 1, d), shared),
                  pl.BlockSpec((1, 1, d), shared),
                  pl.BlockSpec((1, d), lambda b: (0, 0)),
                  pl.BlockSpec(w_kv.shape, lambda b: (0, 0))],
        out_specs=[pl.BlockSpec((1, kw, n), row), pl.BlockSpec((1, n, kw), row)],
        out_shape=[jax.ShapeDtypeStruct((bsz, kw, n), BF16), out],
        compiler_params=_params("arbitrary"),
        name="ctx_kv_proj",
    )(ctx, sc, sh, g_pre.reshape(1, d), w_kv)


def _attn_body(sink_ref, q_ref, km_ref, kp_ref, kn_ref, vm_ref, vp_ref, vn_ref, kc_ref, vc_ref,
               sg_ref, w_ref, g_ref, gt_ref, x_ref, o_ref, kf_ref, vf_ref, att_ref):
    i = pl.program_id(1)
    ni = pl.num_programs(1)
    tq = q_ref.shape[1]
    blk = ATTN_BLOCK
    per_step = tq // blk
    group = q_ref.shape[-1] // HEAD_DIM // N_KV_HEADS

    kf_ref[:, 0:blk] = kp_ref[0]
    kf_ref[:, blk:blk + tq] = km_ref[0]
    kf_ref[:, blk + tq:] = kn_ref[0]
    vf_ref[0:blk, :] = vp_ref[0]
    vf_ref[blk:blk + tq, :] = vm_ref[0]
    vf_ref[blk + tq:, :] = vn_ref[0]

    ri = lax.broadcasted_iota(jnp.int32, (blk, blk), 0)
    cj = lax.broadcasted_iota(jnp.int32, (blk, blk), 1)
    tri_prev = jnp.where(cj >= ri, 0.0, NEG_INF)
    tri_next = jnp.where(cj <= ri, 0.0, NEG_INF)
    nt_dims = (((1,), (1,)), ((), ()))

    for qb in range(per_step):
        r0 = qb * blk
        prev_bias = tri_prev + jnp.where((i * per_step + qb) > 0, 0.0, NEG_INF)
        next_bias = tri_next + jnp.where((i * per_step + qb) < ni * per_step - 1, 0.0, NEG_INF)
        for kh in range(N_KV_HEADS):
            hs = slice(kh * HEAD_DIM, (kh + 1) * HEAD_DIM)
            q4 = jnp.concatenate(
                [q_ref[0, r0:r0 + blk, (kh * group + g) * HEAD_DIM:(kh * group + g + 1) * HEAD_DIM]
                 for g in range(group)], axis=0)
            s_lat = jnp.dot(q4, kf_ref[hs, r0:r0 + 3 * blk], preferred_element_type=F32)
            s_ctx = jnp.dot(q4, kc_ref[0, hs, :], preferred_element_type=F32)
            v_lat = vf_ref[r0:r0 + 3 * blk, hs]
            v_ctx = vc_ref[0, :, hs]
            outs = []
            for g in range(group):
                rows = slice(g * blk, (g + 1) * blk)
                sink = sink_ref[kh * group + g]
                cols = [s_lat[rows, :blk] + prev_bias, s_lat[rows, blk:2 * blk],
                        s_lat[rows, 2 * blk:] + next_bias]
                cols += [s_ctx[rows, c * blk:(c + 1) * blk] for c in range(s_ctx.shape[1] // blk)]
                mx = functools.reduce(jnp.maximum, cols)
                m = jnp.maximum(jnp.max(mx, axis=-1, keepdims=True), sink)
                e = [jnp.exp2(c - m) for c in cols]
                denom = (jnp.sum(functools.reduce(jnp.add, e), axis=-1, keepdims=True)
                         + jnp.exp2(sink - m))
                og = (jnp.dot(jnp.concatenate(e[:3], axis=1).astype(BF16), v_lat,
                              preferred_element_type=F32)
                      + jnp.dot(jnp.concatenate(e[3:], axis=1).astype(BF16), v_ctx,
                                preferred_element_type=F32))
                outs.append(og / denom)
            att_ref[r0:r0 + blk, kh * group * HEAD_DIM:(kh + 1) * group * HEAD_DIM] = (
                jnp.concatenate(outs, axis=1))

    gated = (att_ref[...] * sg_ref[0].astype(F32)).astype(BF16)
    y = jnp.dot(gated, w_ref[...], preferred_element_type=F32)
    o_ref[0] = x_ref[0] + gt_ref[0] * _rms(y, g_ref[...])


def _attention_out(q, k, v, kc, vc, sg, sink, w_out, g_post, gt, x):
    bsz, n, qw = q.shape
    kw = v.shape[-1]
    nc = vc.shape[1]
    d = x.shape[-1]
    blk = ATTN_BLOCK
    tq = _row_tile(n, 512)
    per = tq // blk
    nb = n // blk
    row = lambda b, i: (b, i, 0)
    prev = lambda b, i: (b, jnp.maximum(i * per - 1, 0), 0)
    nxt = lambda b, i: (b, jnp.minimum((i + 1) * per, nb - 1), 0)
    batch = lambda b, i: (b, 0, 0)
    v_specs = [pl.BlockSpec((1, tq, kw), row), pl.BlockSpec((1, blk, kw), prev),
               pl.BlockSpec((1, blk, kw), nxt)]
    k_specs = [pl.BlockSpec((1, kw, tq), lambda b, i: (b, 0, i)),
               pl.BlockSpec((1, kw, blk), lambda b, i: (b, 0, jnp.maximum(i * per - 1, 0))),
               pl.BlockSpec((1, kw, blk), lambda b, i: (b, 0, jnp.minimum((i + 1) * per, nb - 1)))]
    return pl.pallas_call(
        _attn_body,
        grid=(bsz, n // tq),
        in_specs=[pl.BlockSpec(memory_space=pltpu.SMEM),
                  pl.BlockSpec((1, tq, qw), row)]
                 + k_specs + v_specs
                 + [pl.BlockSpec((1, kw, nc), batch), pl.BlockSpec((1, nc, kw), batch),
                    pl.BlockSpec((1, tq, qw), row),
                    pl.BlockSpec(w_out.shape, lambda b, i: (0, 0)),
                    pl.BlockSpec((1, d), lambda b, i: (0, 0)),
                    pl.BlockSpec((1, 1, d), batch),
                    pl.BlockSpec((1, tq, d), row)],
        out_specs=pl.BlockSpec((1, tq, d), row),
        out_shape=jax.ShapeDtypeStruct((bsz, n, d), F32),
        scratch_shapes=[pltpu.VMEM((kw, tq + 2 * blk), BF16),
                        pltpu.VMEM((tq + 2 * blk, kw), BF16),
                        pltpu.VMEM((tq, qw), F32)],
        compiler_params=_params("parallel", "arbitrary"),
        name="window_attention_out",
    )(sink, q, k, k, k, v, v, v, kc, vc, sg, w_out, g_post.reshape(1, d), gt, x)


def _mod_vectors(c, c_ctx, w_mod, b_mod):
    bsz, d = c.shape
    rows = -(-(bsz + 1) // SUBLANES) * SUBLANES
    cvec = jnp.zeros((rows, d), F32).at[:bsz].set(c).at[bsz].set(c_ctx)
    m = _modulation(cvec, w_mod, b_mod)
    lat = [m[:bsz, k * d:(k + 1) * d].reshape(bsz, 1, d) for k in range(3)]
    cx = [m[bsz:bsz + 1, k * d:(k + 1) * d].reshape(1, 1, d) for k in range(3)]
    return lat, cx


def kernel(x, c, ctx, c_ctx, l0_w_mod, l0_b_mod, l0_g_pre, l0_g_post, l0_w_in, l0_w_conv, l0_b_conv,
           l0_w_a, l0_b_a, l0_w_x, l0_b_x, l0_lam, l0_w_f, l0_b_f, l0_w_out,
           l1_w_mod, l1_b_mod, l1_g_pre, l1_g_post, l1_w_in, l1_sink, l1_w_out):
    bsz, n, d = x.shape
    lru_w = l0_w_conv.shape[1]
    f_w = l0_w_f.shape[0] * l0_w_f.shape[1]
    assert lru_w == f_w == d

    (sh_l, sc_l, gt_l), (sh_c, sc_c, gt_c) = _mod_vectors(c, c_ctx, l0_w_mod, l0_b_mod)
    wp, wq = _fold_fourier_weights(l0_w_in, l0_w_f, 2 * lru_w)
    w_cat = jnp.concatenate([l0_w_in[:, :lru_w], 0.5 * l0_w_in[:, lru_w:2 * lru_w], wp, wq,
                             0.5 * l0_w_in[:, 2 * lru_w + f_w:]], axis=1).astype(BF16)
    wg = _gate_weights(l0_w_a, l0_w_x)
    w_out0 = l0_w_out.astype(BF16)

    use_fft = n % (FFT_RADIX * 32) == 0 and n >= 1024
    conv = (l0_w_conv, l0_b_conv)
    u_c, sg_c, p_c, q_c, sgf_c = _l0_in_proj(ctx, sc_c, sh_c, l0_g_pre, w_cat, *conv, False, False)
    u_l, sg_l, p_l, q_l, sgf_l = _l0_in_proj(x, sc_l, sh_l, l0_g_pre, w_cat, *conv, True, use_fft)

    zeros = jnp.zeros((bsz, 1, lru_w), F32)
    lru_args = (wg, l0_b_a, l0_b_x, l0_lam)
    yf_c, yb_c, hf_c, hb_c = _rglru(u_c, *lru_args, zeros, zeros)
    yf_l, yb_l, _, _ = _rglru(u_l, *lru_args, hf_c, hb_c)

    fm_c = _fourier_positions(p_c, q_c, l0_b_f, sgf_c)
    fm_l = (_fourier_positions_fft if use_fft else _fourier_positions)(p_l, q_l, l0_b_f, sgf_l)

    ctx1 = _l0_out_proj(yf_c, yb_c, sg_c, fm_c, w_out0, l0_g_post, gt_c, ctx, False)

    (sh1_l, sc1_l, gt1_l), (sh1_c, sc1_c, _) = _mod_vectors(c, c_ctx, l1_w_mod, l1_b_mod)
    q_width = l1_w_out.shape[0]
    kv_width = (l1_w_in.shape[1] - 2 * q_width) // 2
    w_in1 = jnp.concatenate([l1_w_in[:, :q_width + 2 * kv_width],
                             0.5 * l1_w_in[:, q_width + 2 * kv_width:]], axis=1).astype(BF16)
    kc, vc = _ctx_kv_proj(ctx1, sc1_c, sh1_c, l1_g_pre, w_in1[:, q_width:q_width + 2 * kv_width])
    x1, q, k, v, sg1 = _l0_out_l1_in(yf_l, yb_l, sg_l, fm_l, w_out0, l0_g_post, gt_l, x,
                                     sc1_l, sh1_l, l1_g_pre, w_in1, q_width, kv_width)
    return _attention_out(q, k, v, kc, vc, sg1, l1_sink * LOG2E, l1_w_out.astype(BF16), l1_g_post,
                          gt1_l, x1)
```

```python
import functools
import math

import jax
import jax.numpy as jnp
import numpy as np
from jax import lax
from jax.experimental import pallas as pl
from jax.experimental.pallas import tpu as pltpu

EPS = 1e-6
NEG_INF = -1e30
LRU_C = 8.0
CONV_W = 4
CONV_LEFT = 2
HEAD_DIM = 64
N_KV_HEADS = 4
WINDOW = 128
ATTN_BLOCK = 128
GRID_W = 64
ROPE_BASE = 10000.0
LOG2E = math.log2(math.e)

LANES = 128
SUBLANES = 8
BF16_ROWS = 16
FFT_RADIX = 16
GATE_COLS = 256
FFT_ROW_PARTS = 4
SCAN_SEG = 4
VMEM_LIMIT = 56 * 1024 * 1024

BF16 = jnp.bfloat16
F32 = jnp.float32


def _params(*sem):
    return pltpu.CompilerParams(dimension_semantics=sem, vmem_limit_bytes=VMEM_LIMIT)


def _sigmoid(z):
    return 0.5 * jnp.tanh(0.5 * z) + 0.5


def _silu(z):
    return z * _sigmoid(z)


def _silu_of_double(zh):
    return zh * jnp.tanh(zh) + zh


def _rms(v, g):
    return v * lax.rsqrt(jnp.mean(v * v, axis=-1, keepdims=True) + EPS) * g


def _row_tile(n, want):
    t = min(n, want)
    assert n % t == 0
    return t


def _mod_body(c_ref, w_ref, b_ref, o_ref):
    o_ref[...] = jnp.dot(_silu(c_ref[...]), w_ref[...], preferred_element_type=F32,
                         precision=lax.Precision.HIGHEST) + b_ref[...]


def _modulation(cvec, w_mod, b_mod):
    rows, d = cvec.shape
    width = w_mod.shape[1]
    tn = _row_tile(width, 1024)
    return pl.pallas_call(
        _mod_body,
        grid=(width // tn,),
        in_specs=[pl.BlockSpec((rows, d), lambda j: (0, 0)),
                  pl.BlockSpec((d, tn), lambda j: (0, j)),
                  pl.BlockSpec((1, tn), lambda j: (0, j))],
        out_specs=pl.BlockSpec((rows, tn), lambda j: (0, j)),
        out_shape=jax.ShapeDtypeStruct((rows, width), F32),
        compiler_params=_params("arbitrary"),
        name="modulation",
    )(cvec, w_mod, b_mod.reshape(1, width))


def _fold_body(win_ref, wf_ref, cd_ref, sd_ref, wp_ref, wq_ref):
    hi = lax.Precision.HIGHEST
    win = win_ref[...]
    wf = wf_ref[0]
    cw = jnp.dot(cd_ref[...], wf, preferred_element_type=F32, precision=hi)
    sw = jnp.dot(sd_ref[...], wf, preferred_element_type=F32, precision=hi)
    wp_ref[...] = jnp.dot(win, cw, preferred_element_type=F32, precision=hi)
    wq_ref[...] = jnp.dot(win, sw, preferred_element_type=F32, precision=hi)


def _fold_fourier_weights(w_in, w_f, col0):
    d = w_in.shape[0]
    groups, gd, _ = w_f.shape
    idx = np.arange(gd)
    ang = 2.0 * np.pi * ((idx[:, None] * idx[None, :]) % gd) / gd
    cd = jnp.asarray(np.cos(ang) / math.sqrt(gd), F32)
    sd = jnp.asarray(np.sin(ang) / math.sqrt(gd), F32)
    out = jax.ShapeDtypeStruct((d, groups * gd), F32)
    return pl.pallas_call(
        _fold_body,
        grid=(groups,),
        in_specs=[pl.BlockSpec((d, gd), lambda g: (0, col0 // gd + g)),
                  pl.BlockSpec((1, gd, gd), lambda g: (g, 0, 0)),
                  pl.BlockSpec((gd, gd), lambda g: (0, 0)),
                  pl.BlockSpec((gd, gd), lambda g: (0, 0))],
        out_specs=[pl.BlockSpec((d, gd), lambda g: (0, g)),
                   pl.BlockSpec((d, gd), lambda g: (0, g))],
        out_shape=[out, out],
        compiler_params=_params("arbitrary"),
        name="fold_fourier_weights",
    )(w_in, w_f, cd, sd)


def _l0_in_body(decimate, x_ref, xp_ref, xn_ref, sc_ref, sh_ref, g_ref, w_ref, wconv_ref, bconv_ref,
                u_ref, sg_ref, p_ref, q_ref, sgf_ref, *slab):
    j = pl.program_id(1)
    width = u_ref.shape[-1]
    t = x_ref.shape[1]

    def modulated(v):
        return _rms(v, g_ref[...]) * (1.0 + sc_ref[0]) + sh_ref[0]

    h = modulated(x_ref[0])
    hb = h.astype(BF16)

    h_prev = modulated(xp_ref[0]) * jnp.where(j > 0, 1.0, 0.0)
    h_next = modulated(xn_ref[0]) * jnp.where(j < pl.num_programs(1) - 1, 1.0, 0.0)
    h_ext = jnp.concatenate([h_prev, h, h_next], axis=0).astype(BF16)
    z_ext = jnp.dot(h_ext, w_ref[:, :width], preferred_element_type=F32)
    uc = bconv_ref[...] + jnp.zeros((t, width), F32)
    for k in range(CONV_W):
        shift = (CONV_LEFT - k) % (t + 2 * SUBLANES)
        tap = z_ext if shift == 0 else pltpu.roll(z_ext, shift, 0)
        uc = uc + tap[SUBLANES:SUBLANES + t] * wconv_ref[k:k + 1, :]
    u_ref[0] = uc.astype(u_ref.dtype)

    hb_dec = hb
    if decimate:
        slab_ref, = slab
        per = p_ref.shape[2]
        d = h.shape[-1]
        for c in range(d // LANES):
            slab_ref[c] = h[:, c * LANES:(c + 1) * LANES]
        hb_dec = jnp.concatenate(
            [jnp.concatenate([slab_ref[c, pl.ds(n1, per, stride=FFT_RADIX), :]
                              for c in range(d // LANES)], axis=1)
             for n1 in range(FFT_RADIX)], axis=0).astype(BF16)
    outs = (None, sg_ref, p_ref, q_ref, sgf_ref)
    gated = (False, True, False, False, True)
    for part, (o_ref, is_gate) in enumerate(zip(outs, gated)):
        if o_ref is None:
            continue
        lhs = hb_dec if part in (2, 3) else hb
        z = jnp.dot(lhs, w_ref[:, part * width:(part + 1) * width], preferred_element_type=F32)
        if is_gate:
            z = _silu_of_double(z)
        o_ref[...] = z.astype(o_ref.dtype).reshape(o_ref.shape)


def _l0_in_proj(x, sc, sh, g_pre, w_cat, w_conv, b_conv, per_batch_mod, decimate):
    bsz, n, d = x.shape
    width = w_cat.shape[1] // 5
    t = _row_tile(n, 1024)
    per8 = t // SUBLANES
    last8 = n // SUBLANES - 1
    mod_idx = (lambda b, j: (b, 0, 0)) if per_batch_mod else (lambda b, j: (0, 0, 0))
    out = jax.ShapeDtypeStruct((bsz, n, width), BF16)
    row_spec = pl.BlockSpec((1, t, width), lambda b, j: (b, j, 0))
    if decimate:
        dec = jax.ShapeDtypeStruct((bsz, FFT_RADIX, n // FFT_RADIX, width), BF16)
        dec_spec = pl.BlockSpec((1, FFT_RADIX, t // FFT_RADIX, width), lambda b, j: (b, 0, j, 0))
        out_specs = [row_spec, row_spec, dec_spec, dec_spec, row_spec]
        out_shape = [out, out, dec, dec, out]
        scratch = [pltpu.VMEM((d // LANES, t, LANES), F32)]
    else:
        out_specs, out_shape, scratch = [row_spec] * 5, [out] * 5, []
    return pl.pallas_call(
        functools.partial(_l0_in_body, decimate),
        grid=(bsz, n // t),
        in_specs=[pl.BlockSpec((1, t, d), lambda b, j: (b, j, 0)),
                  pl.BlockSpec((1, SUBLANES, d), lambda b, j: (b, jnp.maximum(j * per8 - 1, 0), 0)),
                  pl.BlockSpec((1, SUBLANES, d),
                               lambda b, j: (b, jnp.minimum((j + 1) * per8, last8), 0)),
                  pl.BlockSpec((1, 1, d), mod_idx),
                  pl.BlockSpec((1, 1, d), mod_idx),
                  pl.BlockSpec((1, d), lambda b, j: (0, 0)),
                  pl.BlockSpec(w_cat.shape, lambda b, j: (0, 0), pipeline_mode=pl.Buffered(1)),
                  pl.BlockSpec(w_conv.shape, lambda b, j: (0, 0)),
                  pl.BlockSpec((1, width), lambda b, j: (0, 0))],
        out_specs=out_specs,
        out_shape=out_shape,
        scratch_shapes=scratch,
        compiler_params=_params("parallel", "arbitrary"),
        name="l0_in_proj",
    )(x, x, x, sc, sh, g_pre.reshape(1, d), w_cat, w_conv, b_conv.reshape(1, width))


def _lru_direction(rev, scan_start, u_ref, wg_ref, ba_ref, bx_ref, lam_ref, h0_ref, y_ref, hfin_ref,
                   a_ref, b_ref, carry_ref):
    _, t, width = u_ref.shape
    d = 1 if rev else 0

    z = -lam_ref[d:d + 1, :]
    c_half = (-0.5 * LRU_C) * (jnp.maximum(z, 0.0) + jnp.log1p(jnp.exp(-jnp.abs(z))))
    ba_half = 0.5 * ba_ref[d:d + 1, :]
    bx_half = 0.5 * bx_ref[d:d + 1, :]

    for cg in range(width // GATE_COLS):
        cols = slice(cg * GATE_COLS, (cg + 1) * GATE_COLS)
        ub = u_ref[0, :, cols]
        zg = jnp.dot(ub, wg_ref[d, cg], preferred_element_type=F32)
        ta = jnp.tanh(zg[:, :GATE_COLS] + ba_half[:, cols])
        ti = jnp.tanh(zg[:, GATE_COLS:] + bx_half[:, cols])
        log_a = c_half[:, cols] * ta + c_half[:, cols]
        a = jnp.exp(log_a)
        v = -jnp.tanh(log_a) * (1.0 + a * a)
        amp = jnp.where(v > 0.0, v * lax.rsqrt(v), 0.0)
        u_half = 0.5 * ub.astype(F32)
        bt = amp * (u_half * ti + u_half)
        for piece in range(GATE_COLS // LANES):
            lanes = slice(piece * LANES, (piece + 1) * LANES)
            a_ref[cg * (GATE_COLS // LANES) + piece] = a[:, lanes]
            b_ref[cg * (GATE_COLS // LANES) + piece] = bt[:, lanes]

    @pl.when(scan_start)
    def _():
        carry_ref[d:d + 1, :] = h0_ref[0]

    slabs = width // LANES
    blk = SCAN_SEG * SUBLANES
    nblk = t // blk
    row = lax.broadcasted_iota(jnp.int32, (SUBLANES, LANES), 0)
    order = tuple(range(SCAN_SEG - 1, -1, -1) if rev else range(SCAN_SEG))

    def body(it, hs):
        g = (nblk - 1 - it) if rev else it
        base = pl.multiple_of(g * blk, blk)
        out = []
        for c in range(slabs):
            h_in = hs[c]
            av = [a_ref[c, pl.ds(base + i, SUBLANES, stride=SCAN_SEG), :] for i in range(SCAN_SEG)]
            bv = [b_ref[c, pl.ds(base + i, SUBLANES, stride=SCAN_SEG), :] for i in range(SCAN_SEG)]
            prod, acc = {order[0]: av[order[0]]}, {order[0]: bv[order[0]]}
            for before, i in zip(order[:-1], order[1:]):
                acc[i] = av[i] * acc[before] + bv[i]
                prod[i] = av[i] * prod[before]
            pc, hc = prod[order[-1]], acc[order[-1]]
            for s in (1, 2, 4):
                if rev:
                    shift, m = SUBLANES - s, row < SUBLANES - s
                else:
                    shift, m = s, row >= s
                p_sh = pltpu.roll(pc, shift, 0)
                h_sh = pltpu.roll(hc, shift, 0)
                hc = jnp.where(m, pc * h_sh + hc, hc)
                pc = jnp.where(m, pc * p_sh, pc)
            h_end = hc + pc * h_in
            if rev:
                h_start = jnp.where(row < SUBLANES - 1, pltpu.roll(h_end, SUBLANES - 1, 0), h_in)
            else:
                h_start = jnp.where(row >= 1, pltpu.roll(h_end, 1, 0), h_in)
            for i in range(SCAN_SEG):
                b_ref[c, pl.ds(base + i, SUBLANES, stride=SCAN_SEG), :] = acc[i] + prod[i] * h_start
            out.append(h_end[0:1, :] if rev else h_end[SUBLANES - 1:SUBLANES, :])
        return tuple(out)

    init = tuple(carry_ref[d:d + 1, c * LANES:(c + 1) * LANES] for c in range(slabs))
    h_last = jnp.concatenate(lax.fori_loop(0, nblk, body, init, unroll=2), axis=1)
    carry_ref[d:d + 1, :] = h_last
    hfin_ref[0] = h_last
    for c in range(slabs):
        y_ref[0, :, c * LANES:(c + 1) * LANES] = b_ref[c].astype(y_ref.dtype)


def _lru_body(uf_ref, ub_ref, wg_ref, ba_ref, bx_ref, lam_ref, h0f_ref, h0b_ref,
              yf_ref, yb_ref, hf_ref, hb_ref, a_ref, b_ref, carry_ref):
    first = pl.program_id(1) == 0
    common = (wg_ref, ba_ref, bx_ref, lam_ref)
    scratch = (a_ref, b_ref, carry_ref)
    _lru_direction(False, first, uf_ref, *common, h0f_ref, yf_ref, hf_ref, *scratch)
    _lru_direction(True, first, ub_ref, *common, h0b_ref, yb_ref, hb_ref, *scratch)


def _rglru(u, wg, b_a, b_x, lam, h0f, h0b):
    bsz, n, width = u.shape
    t = _row_tile(n, 1024)
    nt = n // t
    fwd = lambda b, j: (b, j, 0)
    bwd = lambda b, j: (b, nt - 1 - j, 0)
    tile = (1, t, width)
    const2 = lambda b, j: (0, 0)
    state = pl.BlockSpec((1, 1, width), lambda b, j: (b, 0, 0))
    y_shape = jax.ShapeDtypeStruct((bsz, n, width), BF16)
    h_shape = jax.ShapeDtypeStruct((bsz, 1, width), F32)
    return pl.pallas_call(
        _lru_body,
        grid=(bsz, nt),
        in_specs=[pl.BlockSpec(tile, fwd), pl.BlockSpec(tile, bwd),
                  pl.BlockSpec(wg.shape, lambda b, j: (0, 0, 0, 0)),
                  pl.BlockSpec(b_a.shape, const2),
                  pl.BlockSpec(b_x.shape, const2),
                  pl.BlockSpec(lam.shape, const2),
                  state, state],
        out_specs=[pl.BlockSpec(tile, fwd), pl.BlockSpec(tile, bwd), state, state],
        out_shape=[y_shape, y_shape, h_shape, h_shape],
        scratch_shapes=[pltpu.VMEM((width // LANES, t, LANES), F32),
                        pltpu.VMEM((width // LANES, t, LANES), F32),
                        pltpu.VMEM((2, width), F32)],
        compiler_params=_params("parallel", "arbitrary"),
        name="rglru_scan",
    )(u, u, wg, b_a, b_x, lam, h0f, h0b)


def _gate_weights(w_a, w_x):
    _, heads, blk, _ = w_a.shape
    per = GATE_COLS // blk
    groups = heads // per
    eye = jnp.eye(per, dtype=w_a.dtype)

    def bd(w):
        w = w.reshape(2, groups, per, blk, blk)
        full = jnp.einsum('dgpij,pq->dgpiqj', w, eye)
        return full.reshape(2, groups, GATE_COLS, GATE_COLS)

    return (0.5 * jnp.concatenate([bd(w_a), bd(w_x)], axis=-1)).astype(BF16)


def _dft_body(c_ref, s_ref, p_ref, q_ref, bf_ref, sgf_ref, o_ref, acc_ref):
    k = pl.program_id(2)

    @pl.when(k == 0)
    def _():
        acc_ref[...] = jnp.zeros_like(acc_ref)

    acc_ref[...] += (jnp.dot(c_ref[...], p_ref[0], preferred_element_type=F32)
                     + jnp.dot(s_ref[...], q_ref[0], preferred_element_type=F32))

    @pl.when(k == pl.num_programs(2) - 1)
    def _():
        o_ref[0] = ((acc_ref[...] + bf_ref[...]) * sgf_ref[0].astype(F32)).astype(o_ref.dtype)


def _dft_matrices(n):
    n1 = 1
    while n1 * n1 < n:
        n1 *= 2
    n2 = n // n1
    m = jnp.arange(n, dtype=jnp.int32)[None, :]
    hi = jnp.arange(n2, dtype=jnp.int32)[:, None]
    lo = jnp.arange(n1, dtype=jnp.int32)[:, None]
    ang_hi = ((hi * n1 * m) % n).astype(F32) * (2.0 * math.pi / n)
    ang_lo = ((lo * m) % n).astype(F32) * (2.0 * math.pi / n)
    ch, sh_ = jnp.cos(ang_hi)[:, None, :], jnp.sin(ang_hi)[:, None, :]
    cl, sl = jnp.cos(ang_lo)[None, :, :], jnp.sin(ang_lo)[None, :, :]
    scale = n ** -0.5
    cmat = ((ch * cl - sh_ * sl) * scale).reshape(n, n).astype(BF16)
    smat = ((sh_ * cl + ch * sl) * (-scale)).reshape(n, n).astype(BF16)
    return cmat, smat


def _fourier_positions(p, q, b_f, sgf):
    bsz, n, width = p.shape
    cmat, smat = _dft_matrices(n)
    tm = _row_tile(n, 1024)
    tk = _row_tile(n, 1024)
    return pl.pallas_call(
        _dft_body,
        grid=(bsz, n // tm, n // tk),
        in_specs=[pl.BlockSpec((tm, tk), lambda b, i, k: (i, k)),
                  pl.BlockSpec((tm, tk), lambda b, i, k: (i, k)),
                  pl.BlockSpec((1, tk, width), lambda b, i, k: (b, k, 0)),
                  pl.BlockSpec((1, tk, width), lambda b, i, k: (b, k, 0)),
                  pl.BlockSpec((1, width), lambda b, i, k: (0, 0)),
                  pl.BlockSpec((1, tm, width), lambda b, i, k: (b, i, 0))],
        out_specs=pl.BlockSpec((1, tm, width), lambda b, i, k: (b, i, 0)),
        out_shape=jax.ShapeDtypeStruct((bsz, n, width), BF16),
        scratch_shapes=[pltpu.VMEM((tm, width), F32)],
        compiler_params=_params("parallel", "parallel", "arbitrary"),
        name="fourier_positions",
    )(cmat, smat, p, q, b_f.reshape(1, width), sgf)


def _fft_real(re, im):
    root_half = math.sqrt(0.5)

    def rec(re, im, want_im):
        n = len(re)
        if n == 1:
            return list(re), list(im)
        er, ei = rec(re[0::2], im[0::2], True)
        xr, xi = rec(re[1::2], im[1::2], True)
        out_r, out_i = [None] * n, [None] * n
        for k in range(n // 2):
            if k == 0:
                tr, ti = xr[k], xi[k]
            elif 4 * k == n:
                tr, ti = xi[k], None
            elif 8 * k == n:
                tr = (xr[k] + xi[k]) * root_half
                ti = (xi[k] - xr[k]) * root_half if want_im else None
            elif 8 * k == 3 * n:
                tr = (xi[k] - xr[k]) * root_half
                ti = (xr[k] + xi[k]) * (-root_half) if want_im else None
            else:
                c, s = math.cos(2 * math.pi * k / n), math.sin(2 * math.pi * k / n)
                tr = xr[k] * c + xi[k] * s
                ti = xi[k] * c - xr[k] * s if want_im else None
            out_r[k] = er[k] + tr
            out_r[k + n // 2] = er[k] - tr
            if want_im:
                if 4 * k == n:
                    out_i[k] = ei[k] - xr[k]
                    out_i[k + n // 2] = ei[k] + xr[k]
                else:
                    out_i[k] = ei[k] + ti
                    out_i[k + n // 2] = ei[k] - ti
        return out_r, out_i

    return rec(list(re), list(im), False)[0]


def _fft_body(p_ref, q_ref, wp_ref, wq_ref, bf_ref, sgf_ref, o_ref, yre_ref, yim_ref):
    radix, half, tc = yre_ref.shape
    parts = wp_ref.shape[1]
    hh = half // parts
    rows = BF16_ROWS
    for part in range(parts):
        lo = part * hh
        for n1 in range(radix):
            y = (jnp.dot(wp_ref[n1, part], p_ref[0, n1], preferred_element_type=F32)
                 + jnp.dot(wq_ref[n1, part], q_ref[0, n1], preferred_element_type=F32))
            yre_ref[n1, lo:lo + hh] = y[:hh]
            yim_ref[n1, lo:lo + hh] = y[hh:]
        for r0 in range(lo, lo + hh, rows):
            for lt in range(tc // LANES):
                ls = slice(lt * LANES, (lt + 1) * LANES)
                re = [yre_ref[n1, r0:r0 + rows, ls] for n1 in range(radix)]
                im = [yim_ref[n1, r0:r0 + rows, ls] for n1 in range(radix)]
                out = _fft_real(re, im)
                for k1 in range(radix):
                    gate = sgf_ref[0, k1, r0:r0 + rows, ls].astype(F32)
                    o_ref[0, k1, r0:r0 + rows, ls] = ((out[k1] + bf_ref[:, ls]) * gate).astype(o_ref.dtype)


def _fft_matrices(n):
    half = n // FFT_RADIX
    k2 = np.arange(half, dtype=np.int64)[None, :, None]
    n1 = np.arange(FFT_RADIX, dtype=np.int64)[:, None, None]
    n2 = np.arange(half, dtype=np.int64)[None, None, :]
    ang = ((k2 * (n1 + FFT_RADIX * n2)) % n) * (2.0 * np.pi / n)
    scale = n ** -0.5
    hh = half // FFT_ROW_PARTS
    cm = (np.cos(ang) * scale).reshape(FFT_RADIX, FFT_ROW_PARTS, hh, half)
    sm = (np.sin(ang) * scale).reshape(FFT_RADIX, FFT_ROW_PARTS, hh, half)
    wp = np.concatenate([cm, -sm], axis=2).astype(np.float32)
    wq = np.concatenate([-sm, -cm], axis=2).astype(np.float32)
    return jnp.asarray(wp).astype(BF16), jnp.asarray(wq).astype(BF16)


def _fourier_positions_fft(pd, qd, b_f, sgf):
    bsz, radix, half, width = pd.shape
    n = radix * half
    wp, wq = _fft_matrices(n)
    tc = 2 * LANES
    data = pl.BlockSpec((1, radix, half, tc), lambda b, c: (b, 0, 0, c))
    mats = pl.BlockSpec(wp.shape, lambda b, c: (0, 0, 0, 0))
    out = pl.pallas_call(
        _fft_body,
        grid=(bsz, width // tc),
        in_specs=[data, data, mats, mats,
                  pl.BlockSpec((1, tc), lambda b, c: (0, c)),
                  data],
        out_specs=data,
        out_shape=jax.ShapeDtypeStruct((bsz, radix, half, width), BF16),
        scratch_shapes=[pltpu.VMEM((radix, half, tc), F32), pltpu.VMEM((radix, half, tc), F32)],
        compiler_params=_params("parallel", "arbitrary"),
        name="fourier_positions_fft",
    )(pd, qd, wp, wq, b_f.reshape(1, width), sgf.reshape(bsz, radix, half, width))
    return out.reshape(bsz, n, width)


def _l0_out_body(yf_ref, yb_ref, sg_ref, fm_ref, w_ref, g_ref, gt_ref, x_ref, o_ref):
    width = yf_ref.shape[-1]
    r = (yf_ref[0].astype(F32) + yb_ref[0].astype(F32)) * sg_ref[0].astype(F32)
    y = (jnp.dot(r.astype(BF16), w_ref[:width, :], preferred_element_type=F32)
         + jnp.dot(fm_ref[0], w_ref[width:, :], preferred_element_type=F32))
    o_ref[0] = x_ref[0] + gt_ref[0] * _rms(y, g_ref[...])


def _l0_out_proj(yf, yb, sg, fmix, w_out, g_post, gt, x, per_batch_mod):
    bsz, n, width = yf.shape
    d = x.shape[-1]
    t = _row_tile(n, 512)
    mod_idx = (lambda b, j: (b, 0, 0)) if per_batch_mod else (lambda b, j: (0, 0, 0))
    row = lambda b, j: (b, j, 0)
    return pl.pallas_call(
        _l0_out_body,
        grid=(bsz, n // t),
        in_specs=[pl.BlockSpec((1, t, width), row)] * 4
                 + [pl.BlockSpec(w_out.shape, lambda b, j: (0, 0)),
                    pl.BlockSpec((1, d), lambda b, j: (0, 0)),
                    pl.BlockSpec((1, 1, d), mod_idx),
                    pl.BlockSpec((1, t, d), row)],
        out_specs=pl.BlockSpec((1, t, d), row),
        out_shape=jax.ShapeDtypeStruct((bsz, n, d), F32),
        compiler_params=_params("parallel", "arbitrary"),
        name="l0_out_proj",
    )(yf, yb, sg, fmix, w_out, g_post.reshape(1, d), gt, x)


def _rope(z, cos, sin_signed):
    quarter = HEAD_DIM // 4
    lane = lax.broadcasted_iota(jnp.int32, (z.shape[0], LANES), 1)
    first = (lane % (2 * quarter)) < quarter
    chunks = []
    for c in range(z.shape[1] // LANES):
        zc = z[:, c * LANES:(c + 1) * LANES]
        swapped = jnp.where(first, pltpu.roll(zc, LANES - quarter, 1), pltpu.roll(zc, quarter, 1))
        chunks.append(zc * cos + swapped * sin_signed)
    return jnp.concatenate(chunks, axis=1)


def _l0_out_l1_in_body(yf_ref, yb_ref, sg0_ref, fm_ref, w0_ref, g0_ref, gt0_ref, x_ref,
                       sc_ref, sh_ref, g_ref, w_ref, cos_ref, sin_ref,
                       x1_ref, q_ref, k_ref, v_ref, sg_ref):
    width = yf_ref.shape[-1]
    r = (yf_ref[0].astype(F32) + yb_ref[0].astype(F32)) * sg0_ref[0].astype(F32)
    y = (jnp.dot(r.astype(BF16), w0_ref[:width, :], preferred_element_type=F32)
         + jnp.dot(fm_ref[0], w0_ref[width:, :], preferred_element_type=F32))
    x1 = x_ref[0] + gt0_ref[0] * _rms(y, g0_ref[...])
    x1_ref[0] = x1

    qw, kw = q_ref.shape[-1], v_ref.shape[-1]
    h = _rms(x1, g_ref[...]) * (1.0 + sc_ref[0]) + sh_ref[0]
    hb = h.astype(BF16)
    cos, sin = cos_ref[...], sin_ref[...]
    zq = jnp.dot(hb, w_ref[:, :qw], preferred_element_type=F32)
    q_ref[0] = (_rope(zq, cos, sin) * (HEAD_DIM ** -0.5 * LOG2E)).astype(q_ref.dtype)
    zk = jnp.dot(hb, w_ref[:, qw:qw + kw], preferred_element_type=F32)
    k_ref[0] = _rope(zk, cos, sin).astype(k_ref.dtype)
    zv = jnp.dot(hb, w_ref[:, qw + kw:qw + 2 * kw], preferred_element_type=F32)
    v_ref[0] = zv.astype(v_ref.dtype)
    zg = jnp.dot(hb, w_ref[:, qw + 2 * kw:], preferred_element_type=F32)
    sg_ref[0] = _silu_of_double(zg).astype(sg_ref.dtype)


def _rope_tables(n):
    t = np.arange(n)
    row = (t // GRID_W).astype(np.float64)
    col = (t % GRID_W).astype(np.float64)
    half = HEAD_DIM // 2
    freqs = ROPE_BASE ** (-np.arange(0, half, 2, dtype=np.float64) / half)
    ar = row[:, None] * freqs[None, :]
    ac = col[:, None] * freqs[None, :]
    cos = np.concatenate([np.cos(ar), np.cos(ar), np.cos(ac), np.cos(ac)], axis=1)
    sin = np.concatenate([-np.sin(ar), np.sin(ar), -np.sin(ac), np.sin(ac)], axis=1)
    reps = LANES // HEAD_DIM
    return (jnp.asarray(np.tile(cos, (1, reps)).astype(np.float32)),
            jnp.asarray(np.tile(sin, (1, reps)).astype(np.float32)))


def _l0_out_l1_in(yf, yb, sg0, fmix, w_out0, g_post0, gt0, x, sc, sh, g_pre, w_in, q_width, kv_width):
    bsz, n, width = yf.shape
    d = x.shape[-1]
    t = _row_tile(n, 512)
    cos, sin = _rope_tables(n)
    row = lambda b, j: (b, j, 0)
    mod_idx = lambda b, j: (b, 0, 0)
    const2 = lambda b, j: (0, 0)
    return pl.pallas_call(
        _l0_out_l1_in_body,
        grid=(bsz, n // t),
        in_specs=[pl.BlockSpec((1, t, width), row)] * 4
                 + [pl.BlockSpec(w_out0.shape, const2),
                    pl.BlockSpec((1, d), const2),
                    pl.BlockSpec((1, 1, d), mod_idx),
                    pl.BlockSpec((1, t, d), row),
                    pl.BlockSpec((1, 1, d), mod_idx),
                    pl.BlockSpec((1, 1, d), mod_idx),
                    pl.BlockSpec((1, d), const2),
                    pl.BlockSpec(w_in.shape, const2),
                    pl.BlockSpec((t, LANES), lambda b, j: (j, 0)),
                    pl.BlockSpec((t, LANES), lambda b, j: (j, 0))],
        out_specs=[pl.BlockSpec((1, t, d), row),
                   pl.BlockSpec((1, t, q_width), row),
                   pl.BlockSpec((1, t, kv_width), row),
                   pl.BlockSpec((1, t, kv_width), row),
                   pl.BlockSpec((1, t, q_width), row)],
        out_shape=[jax.ShapeDtypeStruct((bsz, n, d), F32),
                   jax.ShapeDtypeStruct((bsz, n, q_width), BF16),
                   jax.ShapeDtypeStruct((bsz, n, kv_width), BF16),
                   jax.ShapeDtypeStruct((bsz, n, kv_width), BF16),
                   jax.ShapeDtypeStruct((bsz, n, q_width), BF16)],
        compiler_params=_params("parallel", "arbitrary"),
        name="l0_out_l1_in",
    )(yf, yb, sg0, fmix, w_out0, g_post0.reshape(1, d), gt0, x,
      sc, sh, g_pre.reshape(1, d), w_in, cos, sin)


def _ctx_kv_body(x_ref, sc_ref, sh_ref, g_ref, w_ref, k_ref, v_ref):
    kw = v_ref.shape[-1]
    h = _rms(x_ref[0], g_ref[...]) * (1.0 + sc_ref[0]) + sh_ref[0]
    z = jnp.dot(h.astype(BF16), w_ref[...], preferred_element_type=F32)
    k_ref[0] = z[:, :kw].astype(k_ref.dtype)
    v_ref[0] = z[:, kw:].astype(v_ref.dtype)


def _ctx_kv_proj(ctx, sc, sh, g_pre, w_kv):
    bsz, n, d = ctx.shape
    kw = w_kv.shape[1] // 2
    row = lambda b: (b, 0, 0)
    shared = lambda b: (0, 0, 0)
    out = jax.ShapeDtypeStruct((bsz, n, kw), BF16)
    return pl.pallas_call(
        _ctx_kv_body,
        grid=(bsz,),
        in_specs=[pl.BlockSpec((1, n, d), row),
                  pl.BlockSpec((1, 1, d), shared),
                  pl.BlockSpec((1, 1, d), shared),
                  pl.BlockSpec((1, d), lambda b: (0, 0)),
                  pl.BlockSpec(w_kv.shape, lambda b: (0, 0))],
        out_specs=[pl.BlockSpec((1, n, kw), row)] * 2,
        out_shape=[out, out],
        compiler_params=_params("arbitrary"),
        name="ctx_kv_proj",
    )(ctx, sc, sh, g_pre.reshape(1, d), w_kv)


def _attn_body(sink_ref, q_ref, km_ref, kp_ref, kn_ref, vm_ref, vp_ref, vn_ref, kc_ref, vc_ref,
               sg_ref, w_ref, g_ref, gt_ref, x_ref, o_ref, kf_ref, vf_ref, att_ref):
    i = pl.program_id(1)
    ni = pl.num_programs(1)
    tq = q_ref.shape[1]
    blk = ATTN_BLOCK
    per_step = tq // blk
    group = q_ref.shape[-1] // HEAD_DIM // N_KV_HEADS

    kf_ref[0:blk, :] = kp_ref[0]
    kf_ref[blk:blk + tq, :] = km_ref[0]
    kf_ref[blk + tq:, :] = kn_ref[0]
    vf_ref[0:blk, :] = vp_ref[0]
    vf_ref[blk:blk + tq, :] = vm_ref[0]
    vf_ref[blk + tq:, :] = vn_ref[0]

    ri = lax.broadcasted_iota(jnp.int32, (blk, blk), 0)
    cj = lax.broadcasted_iota(jnp.int32, (blk, blk), 1)
    tri_prev = jnp.where(cj >= ri, 0.0, NEG_INF)
    tri_next = jnp.where(cj <= ri, 0.0, NEG_INF)
    nt_dims = (((1,), (1,)), ((), ()))

    for qb in range(per_step):
        r0 = qb * blk
        prev_bias = tri_prev + jnp.where((i * per_step + qb) > 0, 0.0, NEG_INF)
        next_bias = tri_next + jnp.where((i * per_step + qb) < ni * per_step - 1, 0.0, NEG_INF)
        for kh in range(N_KV_HEADS):
            hs = slice(kh * HEAD_DIM, (kh + 1) * HEAD_DIM)
            q4 = jnp.concatenate(
                [q_ref[0, r0:r0 + blk, (kh * group + g) * HEAD_DIM:(kh * group + g + 1) * HEAD_DIM]
                 for g in range(group)], axis=0)
            s_lat = lax.dot_general(q4, kf_ref[r0:r0 + 3 * blk, hs], nt_dims,
                                    preferred_element_type=F32)
            s_ctx = lax.dot_general(q4, kc_ref[0, :, hs], nt_dims, preferred_element_type=F32)
            v_lat = vf_ref[r0:r0 + 3 * blk, hs]
            v_ctx = vc_ref[0, :, hs]
            outs = []
            for g in range(group):
                rows = slice(g * blk, (g + 1) * blk)
                sink = sink_ref[kh * group + g]
                cols = [s_lat[rows, :blk] + prev_bias, s_lat[rows, blk:2 * blk],
                        s_lat[rows, 2 * blk:] + next_bias]
                cols += [s_ctx[rows, c * blk:(c + 1) * blk] for c in range(s_ctx.shape[1] // blk)]
                mx = functools.reduce(jnp.maximum, cols)
                m = jnp.maximum(jnp.max(mx, axis=-1, keepdims=True), sink)
                e = [jnp.exp2(c - m) for c in cols]
                denom = (jnp.sum(functools.reduce(jnp.add, e), axis=-1, keepdims=True)
                         + jnp.exp2(sink - m))
                og = (jnp.dot(jnp.concatenate(e[:3], axis=1).astype(BF16), v_lat,
                              preferred_element_type=F32)
                      + jnp.dot(jnp.concatenate(e[3:], axis=1).astype(BF16), v_ctx,
                                preferred_element_type=F32))
                outs.append(og / denom)
            att_ref[r0:r0 + blk, kh * group * HEAD_DIM:(kh + 1) * group * HEAD_DIM] = (
                jnp.concatenate(outs, axis=1))

    gated = (att_ref[...] * sg_ref[0].astype(F32)).astype(BF16)
    y = jnp.dot(gated, w_ref[...], preferred_element_type=F32)
    o_ref[0] = x_ref[0] + gt_ref[0] * _rms(y, g_ref[...])


def _attention_out(q, k, v, kc, vc, sg, sink, w_out, g_post, gt, x):
    bsz, n, qw = q.shape
    kw = v.shape[-1]
    nc = vc.shape[1]
    d = x.shape[-1]
    blk = ATTN_BLOCK
    assert WINDOW == blk
    tq = _row_tile(n, 512)
    per = tq // blk
    nb = n // blk
    row = lambda b, i: (b, i, 0)
    prev = lambda b, i: (b, jnp.maximum(i * per - 1, 0), 0)
    nxt = lambda b, i: (b, jnp.minimum((i + 1) * per, nb - 1), 0)
    batch = lambda b, i: (b, 0, 0)
    kv_specs = [pl.BlockSpec((1, tq, kw), row), pl.BlockSpec((1, blk, kw), prev),
                pl.BlockSpec((1, blk, kw), nxt)]
    return pl.pallas_call(
        _attn_body,
        grid=(bsz, n // tq),
        in_specs=[pl.BlockSpec(memory_space=pltpu.SMEM),
                  pl.BlockSpec((1, tq, qw), row)]
                 + kv_specs + kv_specs
                 + [pl.BlockSpec((1, nc, kw), batch), pl.BlockSpec((1, nc, kw), batch),
                    pl.BlockSpec((1, tq, qw), row),
                    pl.BlockSpec(w_out.shape, lambda b, i: (0, 0)),
                    pl.BlockSpec((1, d), lambda b, i: (0, 0)),
                    pl.BlockSpec((1, 1, d), batch),
                    pl.BlockSpec((1, tq, d), row)],
        out_specs=pl.BlockSpec((1, tq, d), row),
        out_shape=jax.ShapeDtypeStruct((bsz, n, d), F32),
        scratch_shapes=[pltpu.VMEM((tq + 2 * blk, kw), BF16),
                        pltpu.VMEM((tq + 2 * blk, kw), BF16),
                        pltpu.VMEM((tq, qw), F32)],
        compiler_params=_params("parallel", "arbitrary"),
        name="window_attention_out",
    )(sink, q, k, k, k, v, v, v, kc, vc, sg, w_out, g_post.reshape(1, d), gt, x)


def _mod_vectors(c, c_ctx, w_mod, b_mod):
    bsz, d = c.shape
    rows = -(-(bsz + 1) // SUBLANES) * SUBLANES
    cvec = jnp.zeros((rows, d), F32).at[:bsz].set(c).at[bsz].set(c_ctx)
    m = _modulation(cvec, w_mod, b_mod)
    lat = [m[:bsz, k * d:(k + 1) * d].reshape(bsz, 1, d) for k in range(3)]
    cx = [m[bsz:bsz + 1, k * d:(k + 1) * d].reshape(1, 1, d) for k in range(3)]
    return lat, cx


def kernel(x, c, ctx, c_ctx, l0_w_mod, l0_b_mod, l0_g_pre, l0_g_post, l0_w_in, l0_w_conv, l0_b_conv,
           l0_w_a, l0_b_a, l0_w_x, l0_b_x, l0_lam, l0_w_f, l0_b_f, l0_w_out,
           l1_w_mod, l1_b_mod, l1_g_pre, l1_g_post, l1_w_in, l1_sink, l1_w_out):
    bsz, n, d = x.shape
    lru_w = l0_w_conv.shape[1]
    f_w = l0_w_f.shape[0] * l0_w_f.shape[1]
    assert lru_w == f_w == d

    (sh_l, sc_l, gt_l), (sh_c, sc_c, gt_c) = _mod_vectors(c, c_ctx, l0_w_mod, l0_b_mod)
    wp, wq = _fold_fourier_weights(l0_w_in, l0_w_f, 2 * lru_w)
    w_cat = jnp.concatenate([l0_w_in[:, :lru_w], 0.5 * l0_w_in[:, lru_w:2 * lru_w], wp, wq,
                             0.5 * l0_w_in[:, 2 * lru_w + f_w:]], axis=1).astype(BF16)
    wg = _gate_weights(l0_w_a, l0_w_x)
    w_out0 = l0_w_out.astype(BF16)

    use_fft = n % (FFT_RADIX * 32) == 0 and n >= 1024
    conv = (l0_w_conv, l0_b_conv)
    u_c, sg_c, p_c, q_c, sgf_c = _l0_in_proj(ctx, sc_c, sh_c, l0_g_pre, w_cat, *conv, False, False)
    u_l, sg_l, p_l, q_l, sgf_l = _l0_in_proj(x, sc_l, sh_l, l0_g_pre, w_cat, *conv, True, use_fft)

    zeros = jnp.zeros((bsz, 1, lru_w), F32)
    lru_args = (wg, l0_b_a, l0_b_x, l0_lam)
    yf_c, yb_c, hf_c, hb_c = _rglru(u_c, *lru_args, zeros, zeros)
    yf_l, yb_l, _, _ = _rglru(u_l, *lru_args, hf_c, hb_c)

    fm_c = _fourier_positions(p_c, q_c, l0_b_f, sgf_c)
    fm_l = (_fourier_positions_fft if use_fft else _fourier_positions)(p_l, q_l, l0_b_f, sgf_l)

    ctx1 = _l0_out_proj(yf_c, yb_c, sg_c, fm_c, w_out0, l0_g_post, gt_c, ctx, False)

    (sh1_l, sc1_l, gt1_l), (sh1_c, sc1_c, _) = _mod_vectors(c, c_ctx, l1_w_mod, l1_b_mod)
    q_width = l1_w_out.shape[0]
    kv_width = (l1_w_in.shape[1] - 2 * q_width) // 2
    w_in1 = jnp.concatenate([l1_w_in[:, :q_width + 2 * kv_width],
                             0.5 * l1_w_in[:, q_width + 2 * kv_width:]], axis=1).astype(BF16)
    kc, vc = _ctx_kv_proj(ctx1, sc1_c, sh1_c, l1_g_pre, w_in1[:, q_width:q_width + 2 * kv_width])
    x1, q, k, v, sg1 = _l0_out_l1_in(yf_l, yb_l, sg_l, fm_l, w_out0, l0_g_post, gt_l, x,
                                     sc1_l, sh1_l, l1_g_pre, w_in1, q_width, kv_width)
    return _attention_out(q, k, v, kc, vc, sg1, l1_sink * LOG2E, l1_w_out.astype(BF16), l1_g_post,
                          gt1_l, x1)
```
